```python
import jax, jax.numpy as jnp
from jax import lax
import numpy as np

D_MODEL = 2048
BATCH = 2
SEQ = 4096
DEPTH = 4
DEC_BATCH = 32
DEC_SEQ = 32
PAST_LEN = 4096

CHUNK = 64
N_MIXERS = 3
N_RET = (DEPTH + 2) // 3
N_SWA = (DEPTH + 1) // 3
N_POOL = DEPTH // 3
RET_HEADS = 8
RET_DK = D_MODEL // RET_HEADS
RET_DV = 2 * RET_DK
SWA_HEADS = 32
SWA_KV_HEADS = 8
SWA_GROUP = SWA_HEADS // SWA_KV_HEADS
SWA_HD = D_MODEL // SWA_HEADS
WINDOW = 128
WIN_CHUNKS = WINDOW // CHUNK
POOL_WINDOWS = (2, 4, 8, 16)
POOL_GROUPS = 4
POOL_GD = D_MODEL // POOL_GROUPS
POOL_STATE = 15
N_GROUPS = 4
EXP_PER_GROUP = 4
N_EXPERTS = N_GROUPS * EXP_PER_GROUP
EXP_FF = 512
TOP_K = 2
PLE_DIM = 256
ROPE_THETA = 10000.0
EPS = 1e-6
F32 = jnp.float32

kernel_name = "hybrid_streaming_retention_swa_pool_hmoe_step"


def rms_norm(x, g):
    xf = x.astype(F32)
    y = xf * lax.rsqrt(jnp.mean(xf * xf, axis=-1, keepdims=True) + EPS)
    return (y * g.astype(F32)).astype(x.dtype)


def rope(x, pos):
    half = x.shape[-1] // 2
    inv = 1.0 / jnp.power(ROPE_THETA, jnp.arange(half, dtype=F32) / half)
    ang = pos.astype(F32)[:, None] * inv[None, :]
    cos, sin = jnp.cos(ang)[None, :, None, :], jnp.sin(ang)[None, :, None, :]
    xf = x.astype(F32)
    x1, x2 = xf[..., :half], xf[..., half:]
    return jnp.concatenate([x1 * cos - x2 * sin, x1 * sin + x2 * cos], axis=-1).astype(x.dtype)


def retention_log_decay():
    return jnp.log1p(-jnp.exp2(-5.0 - jnp.arange(RET_HEADS, dtype=F32)))


def retention_block(q, k, v, s_prev, lg):
    l = q.shape[1]
    idx = jnp.arange(l, dtype=F32)
    diff = idx[:, None] - idx[None, :]
    decay = jnp.where(diff >= 0, jnp.exp(jnp.maximum(diff, 0.0)[None] * lg[:, None, None]), 0.0)
    scores = jnp.einsum('blhd,bmhd->bhlm', q, k) * decay[None]
    inner = jnp.einsum('bhlm,bmhe->blhe', scores, v)
    cross = jnp.einsum('blhd,bhde->blhe', q, s_prev) * jnp.exp((idx + 1.0)[:, None] * lg[None, :])[None, :, :, None]
    k_dec = k * jnp.exp((l - 1.0 - idx)[:, None] * lg[None, :])[None, :, :, None]
    s_new = s_prev * jnp.exp(l * lg)[None, :, None, None] + jnp.einsum('blhd,blhe->bhde', k_dec, v)
    return inner + cross, s_new


def retention_mixer(h, pos, s0, w_in, gn_g, w_out):
    b, l, _ = h.shape
    qk, vd = RET_HEADS * RET_DK, RET_HEADS * RET_DV
    q, k, v, g = jnp.split(h @ w_in, [qk, 2 * qk, 2 * qk + vd], axis=-1)
    q = rope(q.reshape(b, l, RET_HEADS, RET_DK), pos).astype(F32) * (RET_DK ** -0.5)
    k = rope(k.reshape(b, l, RET_HEADS, RET_DK), pos).astype(F32)
    v = v.reshape(b, l, RET_HEADS, RET_DV).astype(F32)
    lg = retention_log_decay()
    s0 = s0.astype(F32)
    if l > CHUNK:
        n = l // CHUNK
        to_chunks = lambda t: t.reshape(b, n, CHUNK, *t.shape[2:]).swapaxes(0, 1)

        def step(s, qkv):
            o_c, s = retention_block(qkv[0], qkv[1], qkv[2], s, lg)
            return s, o_c

        s_new, o = lax.scan(step, s0, (to_chunks(q), to_chunks(k), to_chunks(v)))
        o = o.swapaxes(0, 1).reshape(b, l, RET_HEADS, RET_DV)
    else:
        o, s_new = retention_block(q, k, v, s0, lg)
    mu = jnp.mean(o, axis=-1, keepdims=True)
    var = jnp.mean(jnp.square(o - mu), axis=-1, keepdims=True)
    o = ((o - mu) * lax.rsqrt(var + EPS)).reshape(b, l, vd) * gn_g.astype(F32)
    out = (jax.nn.silu(g.astype(F32)) * o).astype(h.dtype) @ w_out
    return out, s_new.astype(h.dtype)


def sink_softmax(logits, valid, sink):
    logits = jnp.where(valid, logits, -jnp.inf)
    m = jnp.maximum(jnp.max(logits, axis=-1, keepdims=True), sink)
    e = jnp.exp(logits - m)
    return e / (jnp.sum(e, axis=-1, keepdims=True) + jnp.exp(sink - m))


def swa_mixer(h, pos, w_in, sink, w_out, ck, cv):
    b, l, _ = h.shape
    qd, kd = SWA_HEADS * SWA_HD, SWA_KV_HEADS * SWA_HD
    q, k, v = jnp.split(h @ w_in, [qd, qd + kd], axis=-1)
    q = rope(q.reshape(b, l, SWA_HEADS, SWA_HD), pos)
    k = rope(k.reshape(b, l, SWA_KV_HEADS, SWA_HD), pos)
    v = v.reshape(b, l, SWA_KV_HEADS, SWA_HD)
    sink = sink.astype(F32).reshape(SWA_KV_HEADS, SWA_GROUP)
    scale = SWA_HD ** -0.5
    if ck is None:
        n = l // CHUNK
        qc = q.reshape(b, n, CHUNK, SWA_KV_HEADS, SWA_GROUP, SWA_HD).astype(F32)
        pad = lambda t: jnp.pad(t.reshape(b, n, CHUNK, SWA_KV_HEADS, SWA_HD),
                                ((0, 0), (WIN_CHUNKS, 0), (0, 0), (0, 0), (0, 0)))
        band = lambda t: jnp.concatenate([t[:, j:j + n] for j in range(WIN_CHUNKS + 1)], axis=2).astype(F32)
        kb, vb = band(pad(k)), band(pad(v))
        s = jnp.einsum('bnqhgd,bnshd->bnhgqs', qc, kb) * scale
        key_chunk = (jnp.arange(n)[:, None] + jnp.arange((WIN_CHUNKS + 1) * CHUNK)[None, :] // CHUNK
                     - WIN_CHUNKS)
        valid = (key_chunk >= 0)[None, :, None, None, None, :]
        pr = sink_softmax(s, valid, sink[None, None, :, :, None, None])
        o = jnp.einsum('bnhgqs,bnshd->bnqhgd', pr, vb).reshape(b, l, qd)
        nk, nv = k[:, -WINDOW:], v[:, -WINDOW:]
    else:
        kk = jnp.concatenate([ck.astype(k.dtype), k], axis=1)
        vv = jnp.concatenate([cv.astype(v.dtype), v], axis=1)
        qg = q.reshape(b, l, SWA_KV_HEADS, SWA_GROUP, SWA_HD).astype(F32)
        s = jnp.einsum('blhgd,bshd->bhgls', qg, kk.astype(F32)) * scale
        pr = sink_softmax(s, True, sink[None, :, :, None, None])
        o = jnp.einsum('bhgls,bshd->blhgd', pr, vv.astype(F32)).reshape(b, l, qd)
        nk, nv = kk[:, -WINDOW:], vv[:, -WINDOW:]
    return o.astype(h.dtype) @ w_out, nk, nv


def pool_mixer(h, hist, pos, w_pool, scale):
    b, l, d = h.shape
    xx = jnp.concatenate([hist.astype(h.dtype), h], axis=1)
    cs = jnp.cumsum(xx.astype(F32), axis=1)
    cs = jnp.concatenate([jnp.zeros((b, 1, d), F32), cs], axis=1)
    end = cs[:, POOL_STATE + 1:]
    hf = h.astype(F32)
    outs = []
    for gi, w in enumerate(POOL_WINDOWS):
        sl = slice(gi * POOL_GD, (gi + 1) * POOL_GD)
        start = cs[:, POOL_STATE + 1 - w:POOL_STATE + 1 - w + l, sl]
        cnt = jnp.minimum(w, pos + 1).astype(F32)[None, :, None]
        outs.append((end[..., sl] - start) / cnt - hf[..., sl])
    pooled = jnp.stack(outs, axis=2).astype(h.dtype)
    mixed = jnp.einsum('blgc,gcd->blgd', pooled, w_pool).reshape(b, l, d)
    return mixed * scale, xx[:, -POOL_STATE:]


def hier_moe(h, w_group, b_group, w_router, b_router, w_gate, w_up, w_down):
    b, l, d = h.shape
    t = h.reshape(b * l, d)
    g_logits = (t @ w_group).astype(F32) + b_group.astype(F32)
    g_idx = jnp.argmax(g_logits, axis=-1)
    g_w = jnp.take_along_axis(jax.nn.softmax(g_logits, axis=-1), g_idx[:, None], axis=-1)
    e_logits = ((t @ w_router).astype(F32) + b_router.astype(F32)).reshape(-1, N_GROUPS, EXP_PER_GROUP)
    e_logits = jnp.take_along_axis(e_logits, g_idx[:, None, None], axis=1)[:, 0]
    top_v, top_i = lax.top_k(e_logits, TOP_K)
    wts = jax.nn.softmax(top_v, axis=-1) * g_w
    eid = g_idx[:, None] * EXP_PER_GROUP + top_i
    comb = jnp.sum(jax.nn.one_hot(eid, N_EXPERTS, dtype=F32) * wts[..., None], axis=1)
    a = jax.nn.silu(jnp.einsum('td,edf->tef', t, w_gate)) * jnp.einsum('td,edf->tef', t, w_up)
    out = jnp.einsum('tef,efd->td', a * comb[..., None].astype(a.dtype), w_down)
    return out.reshape(b, l, d)


def per_layer_embed(hn, p, w_up, w_gate):
    return (p.astype(hn.dtype) @ w_up) * jax.nn.sigmoid(hn @ w_gate)


def trunk(x, p, pos, prm, ret_state, swa_k, swa_v, pool_state):
    b = x.shape[0]
    new_ret, new_k, new_v, new_pool = [], [], [], []
    for i in range(DEPTH):
        kind, j = i % N_MIXERS, i // N_MIXERS
        h = rms_norm(x, prm['norm_mix'][i])
        if kind == 0:
            s0 = jnp.zeros((b, RET_HEADS, RET_DK, RET_DV), x.dtype) if ret_state is None else ret_state[j]
            mix, s_new = retention_mixer(h, pos, s0, prm['ret_w_in'][j], prm['ret_gn'][j], prm['ret_w_out'][j])
            new_ret.append(s_new)
        elif kind == 1:
            ck = None if swa_k is None else swa_k[j]
            cv = None if swa_v is None else swa_v[j]
            mix, nk, nv = swa_mixer(h, pos, prm['swa_w_in'][j], prm['swa_sink'][j], prm['swa_w_out'][j], ck, cv)
            new_k.append(nk)
            new_v.append(nv)
        else:
            hist = jnp.zeros((b, POOL_STATE, D_MODEL), x.dtype) if pool_state is None else pool_state[j]
            mix, nh = pool_mixer(h, hist, pos, prm['pool_w'][j], prm['pool_scale'][j])
            new_pool.append(nh)
        x = x + mix
        x = x + hier_moe(rms_norm(x, prm['norm_ffn'][i]), prm['moe_w_group'][i], prm['moe_b_group'][i],
                         prm['moe_w_router'][i], prm['moe_b_router'][i], prm['moe_w_gate'][i],
                         prm['moe_w_up'][i], prm['moe_w_down'][i])
        x = x + per_layer_embed(rms_norm(x, prm['norm_ple'][i]), p[i], prm['ple_w_up'][i], prm['ple_w_gate'][i])
    y = rms_norm(x, prm['norm_final'])
    return y, jnp.stack(new_ret), jnp.stack(new_k), jnp.stack(new_v), jnp.stack(new_pool)


def setup_inputs(seed: int = 0) -> dict:
    key = jax.random.key(seed)
    ks = iter(jax.random.split(key, 40))
    nrm = lambda shape, scale: jax.random.normal(next(ks), shape, F32) * scale
    d = D_MODEL
    win = min(WINDOW, PAST_LEN)
    return {
        'x_prompt': nrm((BATCH, SEQ, d), 1.0),
        'x_sample': nrm((DEC_BATCH, DEC_SEQ, d), 1.0),
        'state_ret': nrm((N_RET, DEC_BATCH, RET_HEADS, RET_DK, RET_DV), 1.0),
        'cache_swa_k': nrm((N_SWA, DEC_BATCH, win, SWA_KV_HEADS, SWA_HD), 1.0),
        'cache_swa_v': nrm((N_SWA, DEC_BATCH, win, SWA_KV_HEADS, SWA_HD), 1.0),
        'state_pool': nrm((N_POOL, DEC_BATCH, POOL_STATE, d), 1.0),
        'p_prompt': nrm((DEPTH, BATCH, SEQ, PLE_DIM), 1.0),
        'p_sample': nrm((DEPTH, DEC_BATCH, DEC_SEQ, PLE_DIM), 1.0),
        'norm_mix': 1.0 + nrm((DEPTH, d), 0.1),
        'norm_ffn': 1.0 + nrm((DEPTH, d), 0.1),
        'norm_ple': 1.0 + nrm((DEPTH, d), 0.1),
        'norm_final': 1.0 + nrm((d,), 0.1),
        'ret_w_in': nrm((N_RET, d, 2 * RET_HEADS * RET_DK + 2 * RET_HEADS * RET_DV), d ** -0.5),
        'ret_gn': 1.0 + nrm((N_RET, RET_HEADS * RET_DV), 0.1),
        'ret_w_out': nrm((N_RET, RET_HEADS * RET_DV, d), (RET_HEADS * RET_DV) ** -0.5),
        'swa_w_in': nrm((N_SWA, d, (SWA_HEADS + 2 * SWA_KV_HEADS) * SWA_HD), d ** -0.5),
        'swa_sink': nrm((N_SWA, SWA_HEADS), 0.5),
        'swa_w_out': nrm((N_SWA, SWA_HEADS * SWA_HD, d), (SWA_HEADS * SWA_HD) ** -0.5),
        'pool_w': nrm((N_POOL, POOL_GROUPS, POOL_GD, POOL_GD), POOL_GD ** -0.5),
        'pool_scale': 1.0 + nrm((N_POOL, d), 0.1),
        'moe_w_group': nrm((DEPTH, d, N_GROUPS), d ** -0.5),
        'moe_b_group': nrm((DEPTH, N_GROUPS), 0.01),
        'moe_w_router': nrm((DEPTH, d, N_EXPERTS), d ** -0.5),
        'moe_b_router': nrm((DEPTH, N_EXPERTS), 0.01),
        'moe_w_gate': nrm((DEPTH, N_EXPERTS, d, EXP_FF), d ** -0.5),
        'moe_w_up': nrm((DEPTH, N_EXPERTS, d, EXP_FF), d ** -0.5),
        'moe_w_down': nrm((DEPTH, N_EXPERTS, EXP_FF, d), EXP_FF ** -0.5),
        'ple_w_up': nrm((DEPTH, PLE_DIM, d), PLE_DIM ** -0.5),
        'ple_w_gate': nrm((DEPTH, d, d), d ** -0.5),
    }


def reference(x_prompt, x_sample, state_ret, cache_swa_k, cache_swa_v, state_pool, p_prompt, p_sample,
              norm_mix, norm_ffn, norm_ple, norm_final, ret_w_in, ret_gn, ret_w_out, swa_w_in, swa_sink,
              swa_w_out, pool_w, pool_scale, moe_w_group, moe_b_group, moe_w_router, moe_b_router,
              moe_w_gate, moe_w_up, moe_w_down, ple_w_up, ple_w_gate):
    prm = dict(norm_mix=norm_mix, norm_ffn=norm_ffn, norm_ple=norm_ple, norm_final=norm_final,
               ret_w_in=ret_w_in, ret_gn=ret_gn, ret_w_out=ret_w_out, swa_w_in=swa_w_in,
               swa_sink=swa_sink, swa_w_out=swa_w_out, pool_w=pool_w, pool_scale=pool_scale,
               moe_w_group=moe_w_group, moe_b_group=moe_b_group, moe_w_router=moe_w_router,
               moe_b_router=moe_b_router, moe_w_gate=moe_w_gate, moe_w_up=moe_w_up,
               moe_w_down=moe_w_down, ple_w_up=ple_w_up, ple_w_gate=ple_w_gate)
    pos_prompt = jnp.arange(x_prompt.shape[1], dtype=jnp.int32)
    pos_sample = PAST_LEN + jnp.arange(x_sample.shape[1], dtype=jnp.int32)
    y_prompt, ret_p, k_p, v_p, pool_p = trunk(x_prompt, p_prompt, pos_prompt, prm, None, None, None, None)
    y_sample, ret_s, k_s, v_s, pool_s = trunk(x_sample, p_sample, pos_sample, prm, state_ret,
                                              cache_swa_k, cache_swa_v, state_pool)
    return (y_prompt, y_sample, ret_p, ret_s, k_p, k_s, v_p, v_s, pool_p, pool_s)
```

```python
import functools

import jax
import jax.numpy as jnp
from jax import lax
from jax.experimental import pallas as pl
from jax.experimental.pallas import tpu as pltpu

F32 = jnp.float32
BF16 = jnp.bfloat16
EPS = 1e-6
ROPE_THETA = 10000.0
PAST_LEN = 4096
POOL_WINDOWS = (2, 4, 8, 16)
POOL_STATE = POOL_WINDOWS[-1] - 1
HIST_ROWS = 16
VMEM_LIMIT = 56 * 1024 * 1024
EXPERT_TILE = 256
LANES = 128


def _pick(n, cands):
    for c in cands:
        if n % c == 0:
            return c
    raise ValueError(f"no tile in {cands} divides {n}")


def _cparams(sem):
    return pltpu.CompilerParams(dimension_semantics=sem, vmem_limit_bytes=VMEM_LIMIT)


def _rms(x, g):
    return x * lax.rsqrt(jnp.mean(x * x, axis=-1, keepdims=True) + EPS) * g


def _dot(a, b):
    return jnp.dot(a, b, preferred_element_type=F32)


def _dot_nt(a, b):
    return lax.dot_general(a, b, (((1,), (1,)), ((), ())), preferred_element_type=F32)


def _dot_tn(a, b):
    return lax.dot_general(a, b, (((0,), (0,)), ((), ())), preferred_element_type=F32)


def _sigmoid(x):
    return 1.0 / (1.0 + jnp.exp(-x))


def _norm_kernel(x_ref, g_ref, o_ref):
    o_ref[...] = _rms(x_ref[...], g_ref[...]).astype(o_ref.dtype)


def _norm(x, g, out_dtype):
    t, d = x.shape
    tm = _pick(t, (512, 256, 128))
    return pl.pallas_call(
        _norm_kernel,
        grid=(t // tm,),
        in_specs=[pl.BlockSpec((tm, d), lambda i: (i, 0)), pl.BlockSpec((1, d), lambda i: (0, 0))],
        out_specs=pl.BlockSpec((tm, d), lambda i: (i, 0)),
        out_shape=jax.ShapeDtypeStruct((t, d), out_dtype),
        compiler_params=_cparams(("parallel",)),
        name="rmsnorm",
    )(x, g.reshape(1, d))


def _mm_kernel(*refs, has_res, rope):
    a_ref, w_ref = refs[0], refs[1]
    k = 2
    if has_res:
        r_ref = refs[k]
        k += 1
    if rope is not None:
        cos_ref, sin_ref = refs[k], refs[k + 1]
        k += 2
    o_ref = refs[k]
    acc = _dot(a_ref[...], w_ref[...])
    if has_res:
        acc = acc + r_ref[...]
    if rope is None:
        o_ref[...] = acc.astype(o_ref.dtype)
        return
    hd, n_rope_tiles, n_q_tiles, q_scale = rope
    half = hd // 2
    j = pl.program_id(1)

    @pl.when(j < n_rope_tiles)
    def _():
        tn = acc.shape[1]
        rep = tn // cos_ref.shape[1]
        c, s = cos_ref[...], sin_ref[...]
        if rep > 1:
            c, s = jnp.tile(c, (1, rep)), jnp.tile(s, (1, rep))
        lane = lax.broadcasted_iota(jnp.int32, acc.shape, 1)
        partner = jnp.where((lane % hd) < half, pltpu.roll(acc, tn - half, 1), pltpu.roll(acc, half, 1))
        y = acc * c + partner * s
        if q_scale != 1.0:
            y = y * jnp.where(j < n_q_tiles, q_scale, 1.0)
        o_ref[...] = y.astype(o_ref.dtype)

    @pl.when(j >= n_rope_tiles)
    def _():
        o_ref[...] = acc.astype(o_ref.dtype)


def _matmul(a, w, *, tn, res=None, rope=None, tables=None, out_dtype=F32, name="matmul"):
    m, kd = a.shape
    n = w.shape[1]
    tm = _pick(m, (1024, 512, 256, 128))
    in_specs = [pl.BlockSpec((tm, kd), lambda i, j: (i, 0)), pl.BlockSpec((kd, tn), lambda i, j: (0, j))]
    args = [a, w]
    if res is not None:
        in_specs.append(pl.BlockSpec((tm, tn), lambda i, j: (i, j)))
        args.append(res)
    if rope is not None:
        tw = tables[0].shape[1]
        in_specs += [pl.BlockSpec((tm, tw), lambda i, j: (i, 0))] * 2
        args += list(tables)
    return pl.pallas_call(
        functools.partial(_mm_kernel, has_res=res is not None, rope=rope),
        grid=(m // tm, n // tn),
        in_specs=in_specs,
        out_specs=pl.BlockSpec((tm, tn), lambda i, j: (i, j)),
        out_shape=jax.ShapeDtypeStruct((m, n), out_dtype),
        compiler_params=_cparams(("parallel", "arbitrary")),
        name=name,
    )(*args)


def _rope_tables(pos, hd):
    half = hd // 2
    inv = 1.0 / jnp.power(ROPE_THETA, jnp.arange(half, dtype=F32) / half)
    ang = pos.astype(F32)[:, None] * inv[None, :]
    c, s = jnp.cos(ang), jnp.sin(ang)
    cosf = jnp.concatenate([c, c], axis=-1)
    sinf = jnp.concatenate([-s, s], axis=-1)
    rep = max(LANES // hd, 1)
    return jnp.tile(cosf, (1, rep)), jnp.tile(sinf, (1, rep))


def _ret_kernel(lg_ref, q_ref, k_ref, v_ref, g_ref, s0_ref, gn_ref, o_ref, sn_ref, s_scr):
    h, c = pl.program_id(1), pl.program_id(2)
    cl = q_ref.shape[0]

    @pl.when(c == 0)
    def _():
        s_scr[...] = s0_ref[0, 0]

    lg = lg_ref[h]
    q, k, v = q_ref[...], k_ref[...], v_ref[...]
    row = lax.broadcasted_iota(jnp.int32, (cl, cl), 0)
    col = lax.broadcasted_iota(jnp.int32, (cl, cl), 1)
    diff = (row - col).astype(F32)
    decay = jnp.where(diff >= 0, jnp.exp(jnp.maximum(diff, 0.0) * lg), 0.0)
    idx = lax.broadcasted_iota(jnp.int32, (cl, 1), 0).astype(F32)
    qb, vb = q.astype(BF16), v.astype(BF16)
    scores = _dot_nt(qb, k.astype(BF16)) * decay
    inner = _dot(scores.astype(BF16), vb)
    s_prev = s_scr[...]
    cross = _dot(qb, s_prev.astype(BF16)) * jnp.exp((idx + 1.0) * lg)
    k_dec = k * jnp.exp((cl - 1.0 - idx) * lg)
    chunk_decay = jnp.exp(jnp.full((1, 1), float(cl), F32) * lg)
    s_new = s_prev * chunk_decay + _dot_tn(k_dec.astype(BF16), vb)
    s_scr[...] = s_new
    o = inner + cross
    mu = jnp.mean(o, axis=-1, keepdims=True)
    var = jnp.mean(jnp.square(o - mu), axis=-1, keepdims=True)
    on = (o - mu) * lax.rsqrt(var + EPS) * gn_ref[...]
    g = g_ref[...]
    o_ref[...] = ((g * _sigmoid(g)) * on).astype(o_ref.dtype)

    @pl.when(c == pl.num_programs(2) - 1)
    def _():
        sn_ref[0, 0] = s_new


def _retention(qkvg, s0, gn, lg, *, row0, nb, seq, cl):
    _, nh, dk, dv = s0.shape
    assert dv == 2 * dk and row0 % cl == 0 and seq % cl == 0
    nc = seq // cl
    rb0 = row0 // cl
    rowmap = lambda b, c: rb0 + b * nc + c
    grid_spec = pltpu.PrefetchScalarGridSpec(
        num_scalar_prefetch=1,
        grid=(nb, nh, nc),
        in_specs=[
            pl.BlockSpec((cl, dk), lambda b, h, c, lg: (rowmap(b, c), h)),
            pl.BlockSpec((cl, dk), lambda b, h, c, lg: (rowmap(b, c), nh + h)),
            pl.BlockSpec((cl, dv), lambda b, h, c, lg: (rowmap(b, c), nh + h)),
            pl.BlockSpec((cl, dv), lambda b, h, c, lg: (rowmap(b, c), 2 * nh + h)),
            pl.BlockSpec((1, 1, dk, dv), lambda b, h, c, lg: (b, h, 0, 0)),
            pl.BlockSpec((1, dv), lambda b, h, c, lg: (0, h)),
        ],
        out_specs=[
            pl.BlockSpec((cl, dv), lambda b, h, c, lg: (b * nc + c, h)),
            pl.BlockSpec((1, 1, dk, dv), lambda b, h, c, lg: (b, h, 0, 0)),
        ],
        scratch_shapes=[pltpu.VMEM((dk, dv), F32)],
    )
    return pl.pallas_call(
        _ret_kernel,
        grid_spec=grid_spec,
        out_shape=[jax.ShapeDtypeStruct((nb * seq, nh * dv), BF16), jax.ShapeDtypeStruct(s0.shape, F32)],
        compiler_params=_cparams(("parallel", "parallel", "arbitrary")),
        name="retention",
    )(lg, qkvg, qkvg, qkvg, qkvg, s0, gn.reshape(1, nh * dv))


def _attend(sink_ref, q, k_all, v_all, mask, o_ref, *, hk, grp, hd):
    lq = q.shape[0]
    scale = hd ** -0.5
    for kh in range(hk):
        kb = k_all[:, kh * hd:(kh + 1) * hd].astype(BF16)
        vb = v_all[:, kh * hd:(kh + 1) * hd].astype(BF16)
        heads = [kh * grp + g for g in range(grp)]
        qg = jnp.concatenate([q[:, hq * hd:(hq + 1) * hd] for hq in heads], axis=0).astype(BF16)
        s = _dot_nt(qg, kb) * scale
        if mask is not None:
            s = jnp.where(mask, s, -jnp.inf)
        sink = jnp.concatenate([jnp.full((lq, 1), sink_ref[hq], F32) for hq in heads], axis=0)
        m = jnp.maximum(jnp.max(s, axis=-1, keepdims=True), sink)
        e = jnp.exp(s - m)
        pr = e / (jnp.sum(e, axis=-1, keepdims=True) + jnp.exp(sink - m))
        oh = _dot(pr.astype(BF16), vb)
        for g, hq in enumerate(heads):
            o_ref[:, hq * hd:(hq + 1) * hd] = oh[g * lq:(g + 1) * lq].astype(o_ref.dtype)


def _swa_prompt_kernel(sink_ref, q_ref, k0_ref, k1_ref, k2_ref, v0_ref, v1_ref, v2_ref, o_ref, *, hk, grp, hd):
    c = pl.program_id(1)
    cl = q_ref.shape[0]
    k_all = jnp.concatenate([k0_ref[...], k1_ref[...], k2_ref[...]], axis=0)
    v_all = jnp.concatenate([v0_ref[...], v1_ref[...], v2_ref[...]], axis=0)
    key_chunk = c - 2 + lax.broadcasted_iota(jnp.int32, (1, 3 * cl), 1) // cl
    _attend(sink_ref, q_ref[...], k_all, v_all, key_chunk >= 0, o_ref, hk=hk, grp=grp, hd=hd)


def _swa_sample_kernel(sink_ref, q_ref, kn_ref, vn_ref, kc_ref, vc_ref, o_ref, *, hk, grp, hd):
    k_all = jnp.concatenate([kc_ref[0], kn_ref[...]], axis=0)
    v_all = jnp.concatenate([vc_ref[0], vn_ref[...]], axis=0)
    _attend(sink_ref, q_ref[...], k_all, v_all, None, o_ref, hk=hk, grp=grp, hd=hd)


def _swa_prompt(qkv, sink, *, nb, seq, cl, hq, hk, hd):
    nc = seq // cl
    qd, kd = hq * hd, hk * hd
    assert qd % kd == 0
    kcol, vcol = qd // kd, qd // kd + 1
    prev = lambda b, c, j: b * nc + jnp.maximum(c - j, 0)
    kv_specs = [pl.BlockSpec((cl, kd), functools.partial(lambda b, c, s, j, col: (prev(b, c, j), col), j=j, col=col))
                for col in (kcol, vcol) for j in (2, 1, 0)]
    grid_spec = pltpu.PrefetchScalarGridSpec(
        num_scalar_prefetch=1,
        grid=(nb, nc),
        in_specs=[pl.BlockSpec((cl, qd), lambda b, c, s: (b * nc + c, 0))] + kv_specs,
        out_specs=pl.BlockSpec((cl, qd), lambda b, c, s: (b * nc + c, 0)),
    )
    return pl.pallas_call(
        functools.partial(_swa_prompt_kernel, hk=hk, grp=hq // hk, hd=hd),
        grid_spec=grid_spec,
        out_shape=jax.ShapeDtypeStruct((nb * seq, qd), BF16),
        compiler_params=_cparams(("parallel", "arbitrary")),
        name="swa_prompt",
    )(sink, *([qkv] * 7))


def _swa_sample(qkv, sink, ck, cv, *, row0, nb, seq, hq, hk, hd):
    qd, kd = hq * hd, hk * hd
    win = ck.shape[1]
    rb0 = row0 // seq
    grid_spec = pltpu.PrefetchScalarGridSpec(
        num_scalar_prefetch=1,
        grid=(nb,),
        in_specs=[
            pl.BlockSpec((seq, qd), lambda b, s: (rb0 + b, 0)),
            pl.BlockSpec((seq, kd), lambda b, s: (rb0 + b, qd // kd)),
            pl.BlockSpec((seq, kd), lambda b, s: (rb0 + b, qd // kd + 1)),
            pl.BlockSpec((1, win, kd), lambda b, s: (b, 0, 0)),
            pl.BlockSpec((1, win, kd), lambda b, s: (b, 0, 0)),
        ],
        out_specs=pl.BlockSpec((seq, qd), lambda b, s: (b, 0)),
    )
    return pl.pallas_call(
        functools.partial(_swa_sample_kernel, hk=hk, grp=hq // hk, hd=hd),
        grid_spec=grid_spec,
        out_shape=jax.ShapeDtypeStruct((nb * seq, qd), BF16),
        compiler_params=_cparams(("parallel",)),
        name="swa_sample",
    )(sink, qkv, qkv, qkv, ck.reshape(nb, win, kd), cv.reshape(nb, win, kd))


def _pool_kernel(x_ref, g_ref, hist_ref, w_ref, sc_ref, o_ref, st_ref, hb, *, pos0):
    l = pl.program_id(1)
    tm, d = x_ref.shape
    gd = w_ref.shape[1]

    @pl.when(l == 0)
    def _():
        hb[0:HIST_ROWS - POOL_STATE, :] = jnp.zeros((HIST_ROWS - POOL_STATE, d), F32)
        hb[HIST_ROWS - POOL_STATE:HIST_ROWS, :] = hist_ref[0]

    x = x_ref[...]
    hb[HIST_ROWS:HIST_ROWS + tm, :] = _rms(x, g_ref[...])
    pos = pos0 + l * tm + lax.broadcasted_iota(jnp.int32, (tm, 1), 0)
    for gi, w in enumerate(POOL_WINDOWS):
        cs = slice(gi * gd, (gi + 1) * gd)
        hcur = hb[HIST_ROWS:HIST_ROWS + tm, cs]
        acc = hcur
        for j in range(1, w):
            acc = acc + hb[HIST_ROWS - j:HIST_ROWS - j + tm, cs]
        cnt = jnp.minimum(w, pos + 1).astype(F32)
        pooled = acc / cnt - hcur
        mixed = _dot(pooled.astype(BF16), w_ref[gi])
        o_ref[:, cs] = x[:, cs] + mixed * sc_ref[:, cs]
    tail = hb[tm + HIST_ROWS - POOL_STATE:tm + HIST_ROWS, :]
    hb[HIST_ROWS - POOL_STATE:HIST_ROWS, :] = tail

    @pl.when(l == pl.num_programs(1) - 1)
    def _():
        st_ref[0] = tail


def _pool(x, g, hist, w, scale, *, row0, nb, seq, pos0):
    d = x.shape[1]
    ng, gd = w.shape[0], w.shape[1]
    tm = _pick(seq, (512, 256, 128, 64, 32))
    nl = seq // tm
    rb0 = row0 // tm
    assert row0 % tm == 0 and ng == len(POOL_WINDOWS) and tm >= POOL_STATE
    return pl.pallas_call(
        functools.partial(_pool_kernel, pos0=pos0),
        grid=(nb, nl),
        in_specs=[
            pl.BlockSpec((tm, d), lambda b, l: (rb0 + b * nl + l, 0)),
            pl.BlockSpec((1, d), lambda b, l: (0, 0)),
            pl.BlockSpec((1, POOL_STATE, d), lambda b, l: (b, 0, 0)),
            pl.BlockSpec((ng, gd, gd), lambda b, l: (0, 0, 0)),
            pl.BlockSpec((1, d), lambda b, l: (0, 0)),
        ],
        out_specs=[
            pl.BlockSpec((tm, d), lambda b, l: (b * nl + l, 0)),
            pl.BlockSpec((1, POOL_STATE, d), lambda b, l: (b, 0, 0)),
        ],
        out_shape=[jax.ShapeDtypeStruct((nb * seq, d), F32), jax.ShapeDtypeStruct((nb, POOL_STATE, d), F32)],
        scratch_shapes=[pltpu.VMEM((HIST_ROWS + tm, d), F32)],
        compiler_params=_cparams(("parallel", "arbitrary")),
        name="pool_mixer",
    )(x, g.reshape(1, d), hist, w, scale.reshape(1, d))


def _router_kernel(x_ref, g_ref, w_ref, b_ref, h_ref, info_ref, cnt_ref, carry, *, ng, epg):
    i = pl.program_id(0)
    tm = x_ref.shape[0]

    @pl.when(i == 0)
    def _():
        carry[...] = jnp.zeros_like(carry)

    h = _rms(x_ref[...], g_ref[...])
    h_ref[...] = h
    logits = jnp.dot(h, w_ref[...], precision=lax.Precision.HIGHEST, preferred_element_type=F32) + b_ref[...]
    lane = lax.broadcasted_iota(jnp.int32, logits.shape, 1)
    ninf = -jnp.inf
    big = jnp.int32(LANES)
    is_g = lane < ng
    gl = jnp.where(is_g, logits, ninf)
    gmax = jnp.max(gl, axis=-1, keepdims=True)
    gidx = jnp.min(jnp.where(gl == gmax, lane, big), axis=-1, keepdims=True)
    g_w = 1.0 / jnp.sum(jnp.where(is_g, jnp.exp(gl - gmax), 0.0), axis=-1, keepdims=True)
    lo = ng + gidx * epg
    el = jnp.where((lane >= lo) & (lane < lo + epg), logits, ninf)
    v1 = jnp.max(el, axis=-1, keepdims=True)
    i1 = jnp.min(jnp.where(el == v1, lane, big), axis=-1, keepdims=True)
    el2 = jnp.where(lane == i1, ninf, el)
    v2 = jnp.max(el2, axis=-1, keepdims=True)
    i2 = jnp.min(jnp.where(el2 == v2, lane, big), axis=-1, keepdims=True)
    e = jnp.exp(v2 - v1)
    w1 = (1.0 / (1.0 + e)) * g_w
    w2 = (e / (1.0 + e)) * g_w
    e1, e2 = i1 - ng, i2 - ng
    hot = (lane == e1) | (lane == e2)
    row = lax.broadcasted_iota(jnp.int32, (tm, tm), 0)
    col = lax.broadcasted_iota(jnp.int32, (tm, tm), 1)
    before = _dot((col < row).astype(BF16), hot.astype(BF16)) + carry[...]
    r1 = jnp.sum(jnp.where(lane == e1, before, 0.0), axis=-1, keepdims=True)
    r2 = jnp.sum(jnp.where(lane == e2, before, 0.0), axis=-1, keepdims=True)
    fields = (e1.astype(F32), e2.astype(F32), w1, w2, r1, r2)
    info = jnp.zeros(logits.shape, F32)
    for n, f in enumerate(fields):
        info = jnp.where(lane == n, f, info)
    info_ref[...] = info
    total = carry[...] + jnp.sum(hot.astype(F32), axis=0, keepdims=True)
    carry[...] = total
    cnt_ref[...] = total


def _router(x, g, w_cat, b_cat, *, ng, epg):
    t, d = x.shape
    tm = _pick(t, (256, 128))
    return pl.pallas_call(
        functools.partial(_router_kernel, ng=ng, epg=epg),
        grid=(t // tm,),
        in_specs=[
            pl.BlockSpec((tm, d), lambda i: (i, 0)),
            pl.BlockSpec((1, d), lambda i: (0, 0)),
            pl.BlockSpec((d, LANES), lambda i: (0, 0)),
            pl.BlockSpec((1, LANES), lambda i: (0, 0)),
        ],
        out_specs=[
            pl.BlockSpec((tm, d), lambda i: (i, 0)),
            pl.BlockSpec((tm, LANES), lambda i: (i, 0)),
            pl.BlockSpec((1, LANES), lambda i: (0, 0)),
        ],
        out_shape=[jax.ShapeDtypeStruct((t, d), F32), jax.ShapeDtypeStruct((t, LANES), F32),
                   jax.ShapeDtypeStruct((1, LANES), F32)],
        scratch_shapes=[pltpu.VMEM((1, LANES), F32)],
        compiler_params=_cparams(("arbitrary",)),
        name="moe_router",
    )(x, g.reshape(1, d), w_cat, b_cat)


def _row_copy(src_hbm, row, dst, r, sem):
    return pltpu.make_async_copy(src_hbm.at[pl.ds(row, 1), :], dst.at[pl.ds(r, 1), :], sem)


def _experts_kernel(te_ref, src_ref, nu_ref, h_hbm, wg_ref, wu_ref, wd_ref, y_ref, buf, sem):
    i = pl.program_id(0)
    tm = y_ref.shape[0]
    n_used = nu_ref[0]

    def gather(tile, slot, start):
        def body(r, carry):
            cp = _row_copy(h_hbm, src_ref[tile * tm + r], buf.at[slot], r, sem.at[slot])
            if start:
                cp.start()
            else:
                cp.wait()
            return carry
        lax.fori_loop(0, tm, body, 0)

    @pl.when((i == 0) & (n_used > 0))
    def _():
        gather(0, 0, True)

    @pl.when(i + 1 < n_used)
    def _():
        gather(i + 1, (i + 1) % 2, True)

    @pl.when(i < n_used)
    def _():
        slot = i % 2
        gather(i, slot, False)
        xb = buf[slot].astype(BF16)
        gt = _dot(xb, wg_ref[0])
        up = _dot(xb, wu_ref[0])
        a = (gt * _sigmoid(gt)) * up
        y_ref[...] = _dot(a.astype(BF16), wd_ref[0])

    @pl.when(i >= n_used)
    def _():
        y_ref[...] = jnp.zeros_like(y_ref)


def _experts(h, tile_expert, src, n_used, wg, wu, wd):
    d = h.shape[1]
    ne, _, ff = wg.shape
    tm = EXPERT_TILE
    p = src.shape[0]
    grid_spec = pltpu.PrefetchScalarGridSpec(
        num_scalar_prefetch=3,
        grid=(p // tm,),
        in_specs=[
            pl.BlockSpec(memory_space=pl.ANY),
            pl.BlockSpec((1, d, ff), lambda i, te, s, nu: (te[i], 0, 0)),
            pl.BlockSpec((1, d, ff), lambda i, te, s, nu: (te[i], 0, 0)),
            pl.BlockSpec((1, ff, d), lambda i, te, s, nu: (te[i], 0, 0)),
        ],
        out_specs=pl.BlockSpec((tm, d), lambda i, te, s, nu: (i, 0)),
        scratch_shapes=[pltpu.VMEM((2, tm, d), F32), pltpu.SemaphoreType.DMA((2,))],
    )
    return pl.pallas_call(
        _experts_kernel,
        grid_spec=grid_spec,
        out_shape=jax.ShapeDtypeStruct((p, d), F32),
        compiler_params=_cparams(("arbitrary",)),
        name="moe_experts",
    )(tile_expert, src, n_used, h, wg, wu, wd)


def _ple_kernel(p0_ref, p1_ref, x_ref, info_ref, p_ref, y_hbm, g_ref, wup_ref, wgt_ref, gn_ref,
                xo_ref, hn_ref, ybuf, sem):
    i = pl.program_id(0)
    n = pl.num_programs(0)
    tm = x_ref.shape[0]

    def gather(tile, slot, start):
        def body(r, carry):
            for k, pref in enumerate((p0_ref, p1_ref)):
                cp = _row_copy(y_hbm, pref[tile * tm + r], ybuf.at[slot, k], r, sem.at[slot])
                if start:
                    cp.start()
                else:
                    cp.wait()
            return carry
        lax.fori_loop(0, tm, body, 0)

    @pl.when(i == 0)
    def _():
        gather(0, 0, True)

    @pl.when(i + 1 < n)
    def _():
        gather(i + 1, (i + 1) % 2, True)

    slot = i % 2
    gather(i, slot, False)
    info = info_ref[...]
    x2 = x_ref[...] + info[:, 2:3] * ybuf[slot, 0] + info[:, 3:4] * ybuf[slot, 1]
    h3 = _rms(x2, g_ref[...]).astype(BF16)
    pe = _dot(p_ref[...].astype(BF16), wup_ref[...])
    x3 = x2 + pe * _sigmoid(_dot(h3, wgt_ref[...]))
    xo_ref[...] = x3
    hn_ref[...] = _rms(x3, gn_ref[...]).astype(hn_ref.dtype)


def _combine_ple(x, info, p, y, pos0, pos1, g, wup, wgt, gn, hn_dtype):
    t, d = x.shape
    pd = p.shape[1]
    tm = _pick(t, (256, 128))
    const = lambda i, a, b: (0, 0)
    grid_spec = pltpu.PrefetchScalarGridSpec(
        num_scalar_prefetch=2,
        grid=(t // tm,),
        in_specs=[
            pl.BlockSpec((tm, d), lambda i, a, b: (i, 0)),
            pl.BlockSpec((tm, LANES), lambda i, a, b: (i, 0)),
            pl.BlockSpec((tm, pd), lambda i, a, b: (i, 0)),
            pl.BlockSpec(memory_space=pl.ANY),
            pl.BlockSpec((1, d), const),
            pl.BlockSpec((pd, d), const),
            pl.BlockSpec((d, d), const),
            pl.BlockSpec((1, d), const),
        ],
        out_specs=[pl.BlockSpec((tm, d), lambda i, a, b: (i, 0)), pl.BlockSpec((tm, d), lambda i, a, b: (i, 0))],
        scratch_shapes=[pltpu.VMEM((2, 2, tm, d), F32), pltpu.SemaphoreType.DMA((2,))],
    )
    return pl.pallas_call(
        _ple_kernel,
        grid_spec=grid_spec,
        out_shape=[jax.ShapeDtypeStruct((t, d), F32), jax.ShapeDtypeStruct((t, d), hn_dtype)],
        compiler_params=_cparams(("arbitrary",)),
        name="moe_combine_ple",
    )(pos0, pos1, x, info, p, y, g.reshape(1, d), wup, wgt, gn.reshape(1, d))


def _moe_ple(x1, p, g_ffn, w_group, b_group, w_router, b_router, w_gate, w_up, w_down, g_ple, ple_up, ple_gate,
             g_next, hn_dtype):
    t, d = x1.shape
    ng = w_group.shape[1]
    ne = w_router.shape[1]
    epg = ne // ng
    assert ng + ne <= LANES
    w_cat = jnp.zeros((d, LANES), F32).at[:, :ng].set(w_group).at[:, ng:ng + ne].set(w_router)
    b_cat = jnp.zeros((1, LANES), F32).at[0, :ng].set(b_group).at[0, ng:ng + ne].set(b_router)
    h2, info, cnt = _router(x1, g_ffn, w_cat, b_cat, ng=ng, epg=epg)

    tm = EXPERT_TILE
    n_tiles = (2 * t) // tm + ne
    counts = cnt[0, :ne].astype(jnp.int32)
    padded = ((counts + tm - 1) // tm) * tm
    ends = jnp.cumsum(padded)
    offs = ends - padded
    eid = info[:, 0:2].astype(jnp.int32)
    rank = info[:, 4:6].astype(jnp.int32)
    pos = offs[eid] + rank
    n_used = (ends[-1] // tm).astype(jnp.int32).reshape(1)
    tile_start = jnp.arange(n_tiles, dtype=jnp.int32) * tm
    tile_expert = jnp.minimum(jnp.searchsorted(ends, tile_start, side="right"), ne - 1).astype(jnp.int32)
    tok = jnp.broadcast_to(jnp.arange(t, dtype=jnp.int32)[:, None], (t, 2))
    src = jnp.zeros((n_tiles * tm,), jnp.int32).at[pos.reshape(-1)].set(tok.reshape(-1))

    y = _experts(h2, tile_expert, src, n_used, w_gate, w_up, w_down)
    return _combine_ple(x1, info, p, y, pos[:, 0], pos[:, 1], g_ple, ple_up, ple_gate, g_next, hn_dtype)


def kernel(x_prompt, x_sample, state_ret, cache_swa_k, cache_swa_v, state_pool, p_prompt, p_sample, norm_mix, norm_ffn, norm_ple, norm_final, ret_w_in, ret_gn, ret_w_out, swa_w_in, swa_sink, swa_w_out, pool_w, pool_scale, moe_w_group, moe_b_group, moe_w_router, moe_b_router, moe_w_gate, moe_w_up, moe_w_down, ple_w_up, ple_w_gate):
    bp, lp, d = x_prompt.shape
    bs, ls, _ = x_sample.shape
    tp, ts = bp * lp, bs * ls
    depth = norm_mix.shape[0]
    pd = p_prompt.shape[-1]
    _, _, nh, dk, dv = state_ret.shape
    _, _, win, hk, hd = cache_swa_k.shape
    hq = swa_sink.shape[1]
    chunk = 64
    assert ls <= chunk and ls <= win and lp % chunk == 0 and lp >= win

    x = jnp.concatenate([x_prompt.reshape(tp, d), x_sample.reshape(ts, d)], axis=0)
    pos = jnp.concatenate([jnp.tile(jnp.arange(lp, dtype=jnp.int32), bp),
                           jnp.tile(PAST_LEN + jnp.arange(ls, dtype=jnp.int32), bs)])
    lg = jnp.log1p(-jnp.exp2(-5.0 - jnp.arange(nh, dtype=F32)))
    ret_cl = _pick(lp, (256, 128, 64))

    hn = _norm(x, norm_mix[0], BF16)
    ret_p, ret_s, k_p, k_s, v_p, v_s, pool_p, pool_s = [], [], [], [], [], [], [], []
    for i in range(depth):
        kind, j = i % 3, i // 3
        if kind == 0:
            qk = nh * dk
            qkvg = _matmul(hn, ret_w_in[j].astype(BF16), tn=dk, rope=(dk, 2 * nh, nh, dk ** -0.5),
                           tables=_rope_tables(pos, dk), name="ret_in_proj")
            o_p, s_p = _retention(qkvg, jnp.zeros((bp, nh, dk, dv), F32), ret_gn[j], lg,
                                  row0=0, nb=bp, seq=lp, cl=ret_cl)
            o_s, s_s = _retention(qkvg, state_ret[j], ret_gn[j], lg, row0=tp, nb=bs, seq=ls, cl=ls)
            ret_p.append(s_p)
            ret_s.append(s_s)
            o = jnp.concatenate([o_p, o_s], axis=0)
            x1 = _matmul(o, ret_w_out[j].astype(BF16), tn=_pick(d, (512, 256, 128)), res=x, name="ret_out_proj")
        elif kind == 1:
            qd, kd = hq * hd, hk * hd
            tn = _pick(kd, (512, 256, 128))
            qkv = _matmul(hn, swa_w_in[j].astype(BF16), tn=tn, rope=(hd, (qd + kd) // tn, 0, 1.0),
                          tables=_rope_tables(pos, hd), name="swa_in_proj")
            o_p = _swa_prompt(qkv, swa_sink[j], nb=bp, seq=lp, cl=chunk, hq=hq, hk=hk, hd=hd)
            o_s = _swa_sample(qkv, swa_sink[j], cache_swa_k[j], cache_swa_v[j], row0=tp, nb=bs, seq=ls,
                              hq=hq, hk=hk, hd=hd)
            kv_p = qkv[:tp].reshape(bp, lp, qd + 2 * kd)[:, lp - win:, qd:]
            kv_s = qkv[tp:].reshape(bs, ls, qd + 2 * kd)[:, :, qd:]
            k_p.append(kv_p[..., :kd].reshape(bp, win, hk, hd))
            v_p.append(kv_p[..., kd:].reshape(bp, win, hk, hd))
            k_s.append(jnp.concatenate([cache_swa_k[j], kv_s[..., :kd].reshape(bs, ls, hk, hd)], axis=1)[:, -win:])
            v_s.append(jnp.concatenate([cache_swa_v[j], kv_s[..., kd:].reshape(bs, ls, hk, hd)], axis=1)[:, -win:])
            o = jnp.concatenate([o_p, o_s], axis=0)
            x1 = _matmul(o, swa_w_out[j].astype(BF16), tn=_pick(d, (512, 256, 128)), res=x, name="swa_out_proj")
        else:
            wp = pool_w[j].astype(BF16)
            x1_p, st_p = _pool(x, norm_mix[i], jnp.zeros((bp, POOL_STATE, d), F32), wp, pool_scale[j],
                               row0=0, nb=bp, seq=lp, pos0=0)
            x1_s, st_s = _pool(x, norm_mix[i], state_pool[j], wp, pool_scale[j],
                               row0=tp, nb=bs, seq=ls, pos0=PAST_LEN)
            pool_p.append(st_p)
            pool_s.append(st_s)
            x1 = jnp.concatenate([x1_p, x1_s], axis=0)
        p = jnp.concatenate([p_prompt[i].reshape(tp, pd), p_sample[i].reshape(ts, pd)], axis=0)
        last = i == depth - 1
        g_next = norm_final if last else norm_mix[i + 1]
        x, hn = _moe_ple(x1, p, norm_ffn[i], moe_w_group[i], moe_b_group[i], moe_w_router[i], moe_b_router[i],
                         moe_w_gate[i].astype(BF16), moe_w_up[i].astype(BF16), moe_w_down[i].astype(BF16),
                         norm_ple[i], ple_w_up[i].astype(BF16), ple_w_gate[i].astype(BF16),
                         g_next, F32 if last else BF16)
    y = hn
    return (y[:tp].reshape(bp, lp, d), y[tp:].reshape(bs, ls, d),
            jnp.stack(ret_p), jnp.stack(ret_s), jnp.stack(k_p), jnp.stack(k_s),
            jnp.stack(v_p), jnp.stack(v_s), jnp.stack(pool_p), jnp.stack(pool_s))
```

```python
import functools

import jax
import jax.numpy as jnp
from jax import lax
from jax.experimental import pallas as pl
from jax.experimental.pallas import tpu as pltpu

F32 = jnp.float32
BF16 = jnp.bfloat16
EPS = 1e-6
ROPE_THETA = 10000.0
PAST_LEN = 4096
CHUNK = 64
POOL_WINDOWS = (2, 4, 8, 16)
POOL_STATE = POOL_WINDOWS[-1] - 1
HIST_ROWS = 16
VMEM_LIMIT = 56 * 1024 * 1024
EXPERT_TILE = 256
LANES = 128


def _pick(n, cands):
    for c in cands:
        if n % c == 0:
            return c
    raise ValueError(f"no tile in {cands} divides {n}")


def _cparams(sem):
    return pltpu.CompilerParams(dimension_semantics=sem, vmem_limit_bytes=VMEM_LIMIT)


def _rms(x, g):
    return x * lax.rsqrt(jnp.mean(x * x, axis=-1, keepdims=True) + EPS) * g


def _dot(a, b):
    return jnp.dot(a, b, preferred_element_type=F32)


def _dot_nt(a, b):
    return lax.dot_general(a, b, (((1,), (1,)), ((), ())), preferred_element_type=F32)


def _dot_tn(a, b):
    return lax.dot_general(a, b, (((0,), (0,)), ((), ())), preferred_element_type=F32)


def _sigmoid(x):
    return 1.0 / (1.0 + jnp.exp(-x))


_ANY = pl.BlockSpec(memory_space=pl.ANY)


def _norm_kernel(x_ref, g_ref, o_ref):
    o_ref[...] = _rms(x_ref[...], g_ref[...]).astype(o_ref.dtype)


def _norm(x, g, out_dtype):
    t, d = x.shape
    tm = _pick(t, (512, 256, 128))
    return pl.pallas_call(
        _norm_kernel,
        grid=(t // tm,),
        in_specs=[pl.BlockSpec((tm, d), lambda i: (i, 0)), pl.BlockSpec((1, d), lambda i: (0, 0))],
        out_specs=pl.BlockSpec((tm, d), lambda i: (i, 0)),
        out_shape=jax.ShapeDtypeStruct((t, d), out_dtype),
        compiler_params=_cparams(("parallel",)),
        name="rmsnorm",
    )(x, g.reshape(1, d))


def _mm_kernel(*refs, has_res, rope):
    a_ref, w_ref = refs[0], refs[1]
    k = 2
    if has_res:
        r_ref = refs[k]
        k += 1
    if rope is not None:
        cos_ref, sin_ref = refs[k], refs[k + 1]
        k += 2
    o_ref = refs[k]
    acc = _dot(a_ref[...], w_ref[...].astype(BF16))
    if has_res:
        acc = acc + r_ref[...]
    if rope is None:
        o_ref[...] = acc.astype(o_ref.dtype)
        return
    hd, n_rope_tiles, n_q_tiles, q_scale = rope
    half = hd // 2
    j = pl.program_id(1)

    @pl.when(j < n_rope_tiles)
    def _():
        tn = acc.shape[1]
        rep = tn // cos_ref.shape[1]
        c, s = cos_ref[...], sin_ref[...]
        if rep > 1:
            c, s = jnp.tile(c, (1, rep)), jnp.tile(s, (1, rep))
        lane = lax.broadcasted_iota(jnp.int32, acc.shape, 1)
        partner = jnp.where((lane % hd) < half, pltpu.roll(acc, tn - half, 1), pltpu.roll(acc, half, 1))
        y = acc * c + partner * s
        if q_scale != 1.0:
            y = y * jnp.where(j < n_q_tiles, q_scale, 1.0)
        o_ref[...] = y.astype(o_ref.dtype)

    @pl.when(j >= n_rope_tiles)
    def _():
        o_ref[...] = acc.astype(o_ref.dtype)


def _matmul(a, w, layer, *, tn, col0=0, ncols=None, res=None, rope=None, tables=None, out_dtype=F32, name="matmul"):
    m, kd = a.shape
    n = w.shape[2] - col0 if ncols is None else ncols
    assert col0 % tn == 0 and n % tn == 0
    cb0 = col0 // tn
    tm = _pick(m, (1024, 512, 256, 128))
    in_specs = [pl.BlockSpec((tm, kd), lambda i, j: (i, 0)),
                pl.BlockSpec((None, kd, tn), lambda i, j: (layer, 0, cb0 + j))]
    args = [a, w]
    if res is not None:
        in_specs.append(pl.BlockSpec((tm, tn), lambda i, j: (i, j)))
        args.append(res)
    if rope is not None:
        tw = tables[0].shape[1]
        in_specs += [pl.BlockSpec((tm, tw), lambda i, j: (i, 0))] * 2
        args += list(tables)
    return pl.pallas_call(
        functools.partial(_mm_kernel, has_res=res is not None, rope=rope),
        grid=(m // tm, n // tn),
        in_specs=in_specs,
        out_specs=pl.BlockSpec((tm, tn), lambda i, j: (i, j)),
        out_shape=jax.ShapeDtypeStruct((m, n), out_dtype),
        compiler_params=_cparams(("parallel", "arbitrary")),
        name=name,
    )(*args)


def _rope_tables(pos, hd):
    half = hd // 2
    inv = 1.0 / jnp.power(ROPE_THETA, jnp.arange(half, dtype=F32) / half)
    ang = pos.astype(F32)[:, None] * inv[None, :]
    c, s = jnp.cos(ang), jnp.sin(ang)
    cosf = jnp.concatenate([c, c], axis=-1)
    sinf = jnp.concatenate([-s, s], axis=-1)
    rep = max(LANES // hd, 1)
    return jnp.tile(cosf, (1, rep)), jnp.tile(sinf, (1, rep))


def _ret_kernel(*refs, has_s0, n_alias):
    lg_ref, q_ref, k_ref, v_ref, g_ref = refs[:5]
    k0 = 5
    if has_s0:
        s0_ref = refs[k0]
        k0 += 1
    gn_ref = refs[k0]
    o_ref, sn_ref, s_scr, dec_scr = refs[k0 + 1 + n_alias:]
    h, c = pl.program_id(1), pl.program_id(2)
    cl = q_ref.shape[0]
    lg = lg_ref[h]

    @pl.when(c == 0)
    def _():
        s_scr[...] = s0_ref[...] if has_s0 else jnp.zeros_like(s_scr)
        row = lax.broadcasted_iota(jnp.int32, (cl, cl), 0)
        col = lax.broadcasted_iota(jnp.int32, (cl, cl), 1)
        diff = (row - col).astype(F32)
        dec_scr[...] = jnp.where(diff >= 0, jnp.exp(jnp.maximum(diff, 0.0) * lg), 0.0)

    qb, kb, vb = q_ref[...], k_ref[...], v_ref[...]
    idx = lax.broadcasted_iota(jnp.int32, (cl, 1), 0).astype(F32)
    scores = _dot_nt(qb, kb) * dec_scr[...]
    inner = _dot(scores.astype(BF16), vb)
    s_prev = s_scr[...]
    cross = _dot(qb, s_prev.astype(BF16)) * jnp.exp((idx + 1.0) * lg)
    k_dec = kb.astype(F32) * jnp.exp((cl - 1.0 - idx) * lg)
    chunk_decay = jnp.exp(jnp.full((1, 1), float(cl), F32) * lg)
    s_new = s_prev * chunk_decay + _dot_tn(k_dec.astype(BF16), vb)
    s_scr[...] = s_new
    o = inner + cross
    mu = jnp.mean(o, axis=-1, keepdims=True)
    var = jnp.mean(jnp.square(o - mu), axis=-1, keepdims=True)
    on = (o - mu) * lax.rsqrt(var + EPS) * gn_ref[...]
    g = g_ref[...]
    o_ref[...] = ((g * _sigmoid(g)) * on).astype(o_ref.dtype)

    @pl.when(c == pl.num_programs(2) - 1)
    def _():
        sn_ref[...] = s_new


def _retention(qkv, gate, s0_all, gn, lg, *, layer, n_layers, row0, nb, seq, cl, o_buf=None, st_buf=None):
    t = qkv.shape[0]
    nh = lg.shape[0]
    dv = gate.shape[1] // nh
    dk = (qkv.shape[1] - nh * dv) // (2 * nh)
    assert dv == 2 * dk and row0 % cl == 0 and seq % cl == 0
    nc = seq // cl
    rb0 = row0 // cl
    rowmap = lambda b, c: rb0 + b * nc + c
    st_spec = pl.BlockSpec((None, None, None, dk, dv), lambda b, h, c, lg: (layer, b, h, 0, 0))
    in_specs = [
        pl.BlockSpec((cl, dk), lambda b, h, c, lg: (rowmap(b, c), h)),
        pl.BlockSpec((cl, dk), lambda b, h, c, lg: (rowmap(b, c), nh + h)),
        pl.BlockSpec((cl, dv), lambda b, h, c, lg: (rowmap(b, c), nh + h)),
        pl.BlockSpec((cl, dv), lambda b, h, c, lg: (rowmap(b, c), h)),
    ]
    args = [lg, qkv, qkv, qkv, gate]
    if s0_all is not None:
        in_specs.append(st_spec)
        args.append(s0_all)
    in_specs.append(pl.BlockSpec((1, dv), lambda b, h, c, lg: (0, h)))
    args.append(gn.reshape(1, nh * dv))
    aliases = {}
    for out_idx, buf in enumerate((o_buf, st_buf)):
        if buf is not None:
            aliases[len(args)] = out_idx
            in_specs.append(_ANY)
            args.append(buf)
    grid_spec = pltpu.PrefetchScalarGridSpec(
        num_scalar_prefetch=1,
        grid=(nb, nh, nc),
        in_specs=in_specs,
        out_specs=[pl.BlockSpec((cl, dv), lambda b, h, c, lg: (rowmap(b, c), h)), st_spec],
        scratch_shapes=[pltpu.VMEM((dk, dv), F32), pltpu.VMEM((cl, cl), F32)],
    )
    return pl.pallas_call(
        functools.partial(_ret_kernel, has_s0=s0_all is not None, n_alias=len(aliases)),
        grid_spec=grid_spec,
        out_shape=[jax.ShapeDtypeStruct((t, nh * dv), BF16), jax.ShapeDtypeStruct((n_layers, nb, nh, dk, dv), F32)],
        input_output_aliases=aliases,
        compiler_params=_cparams(("parallel", "parallel", "arbitrary")),
        name="retention",
    )(*args)


def _attend(sink_ref, q_ref, k_all, v_all, bias, o_ref, *, hk, grp, hd):
    assert 2 * hd == LANES and grp % 2 == 0 and hk % 2 == 0
    lq = q_ref.shape[0]
    scale = hd ** -0.5
    lane = lax.broadcasted_iota(jnp.int32, (lq, LANES), 1)
    if bias is not None:
        bias = jnp.concatenate([bias] * grp, axis=0)
    for kp in range(hk // 2):
        kb = k_all[:, kp * LANES:(kp + 1) * LANES]
        vb = v_all[:, kp * LANES:(kp + 1) * LANES]
        for a in range(2):
            kh = 2 * kp + a
            own = (lane // hd) == a
            rows = []
            for g in range(grp):
                hq = kh * grp + g
                blk = q_ref[:, (hq // 2) * LANES:(hq // 2 + 1) * LANES]
                if hq % 2 != a:
                    blk = pltpu.roll(blk, hd, 1)
                rows.append(jnp.where(own, blk, 0.0))
            qs = jnp.concatenate(rows, axis=0).astype(BF16)
            s = _dot_nt(qs, kb) * scale
            if bias is not None:
                s = s + bias
            sink = jnp.concatenate([jnp.full((lq, 1), sink_ref[kh * grp + g], F32) for g in range(grp)], axis=0)
            m = jnp.maximum(jnp.max(s, axis=-1, keepdims=True), sink)
            e = jnp.exp(s - m)
            pr = e / (jnp.sum(e, axis=-1, keepdims=True) + jnp.exp(sink - m))
            oh = _dot(pr.astype(BF16), vb)
            for g in range(0, grp, 2):
                hq = kh * grp + g
                lo, hi = oh[g * lq:(g + 1) * lq], oh[(g + 1) * lq:(g + 2) * lq]
                if a == 1:
                    lo = pltpu.roll(lo, hd, 1)
                else:
                    hi = pltpu.roll(hi, hd, 1)
                o_ref[:, (hq // 2) * LANES:(hq // 2 + 1) * LANES] = jnp.where(lane < hd, lo, hi).astype(o_ref.dtype)


def _swa_prompt_kernel(sink_ref, q_ref, kp_ref, kc_ref, vp_ref, vc_ref, o_ref, *, hk, grp, hd, win_chunks):
    c = pl.program_id(1)
    lq = q_ref.shape[0]
    nqb = lq // CHUNK
    k_all = jnp.concatenate([kp_ref[...], kc_ref[...]], axis=0).astype(BF16)
    v_all = jnp.concatenate([vp_ref[...], vc_ref[...]], axis=0).astype(BF16)
    qc = lax.broadcasted_iota(jnp.int32, (lq, 2 * lq), 0) // CHUNK
    kc = lax.broadcasted_iota(jnp.int32, (lq, 2 * lq), 1) // CHUNK
    back = qc + nqb - kc
    valid = (back >= 0) & (back <= win_chunks) & ((c - 1) * nqb + kc >= 0)
    bias = jnp.where(valid, 0.0, -jnp.inf).astype(F32)
    _attend(sink_ref, q_ref, k_all, v_all, bias, o_ref, hk=hk, grp=grp, hd=hd)


def _swa_sample_kernel(sink_ref, q_ref, kn_ref, vn_ref, kc_ref, vc_ref, o_alias, o_ref, *, hk, grp, hd):
    del o_alias
    k_all = jnp.concatenate([kc_ref[0], kn_ref[...]], axis=0).astype(BF16)
    v_all = jnp.concatenate([vc_ref[0], vn_ref[...]], axis=0).astype(BF16)
    _attend(sink_ref, q_ref, k_all, v_all, None, o_ref, hk=hk, grp=grp, hd=hd)


def _swa_prompt(qkv, sink, *, nb, seq, win, hq, hk, hd):
    t = qkv.shape[0]
    qd, kd = hq * hd, hk * hd
    win_chunks = win // CHUNK
    lq = _pick(seq, (2 * CHUNK, CHUNK))
    assert qd % kd == 0 and lq // CHUNK >= win_chunks
    nblk = seq // lq
    kcol, vcol = qd // kd, qd // kd + 1
    cur = lambda col: (lambda b, c, s: (b * nblk + c, col))
    prev = lambda col: (lambda b, c, s: (b * nblk + jnp.maximum(c - 1, 0), col))
    grid_spec = pltpu.PrefetchScalarGridSpec(
        num_scalar_prefetch=1,
        grid=(nb, nblk),
        in_specs=[pl.BlockSpec((lq, qd), cur(0)),
                  pl.BlockSpec((lq, kd), prev(kcol)), pl.BlockSpec((lq, kd), cur(kcol)),
                  pl.BlockSpec((lq, kd), prev(vcol)), pl.BlockSpec((lq, kd), cur(vcol))],
        out_specs=pl.BlockSpec((lq, qd), cur(0)),
    )
    return pl.pallas_call(
        functools.partial(_swa_prompt_kernel, hk=hk, grp=hq // hk, hd=hd, win_chunks=win_chunks),
        grid_spec=grid_spec,
        out_shape=jax.ShapeDtypeStruct((t, qd), BF16),
        compiler_params=_cparams(("parallel", "arbitrary")),
        name="swa_prompt",
    )(sink, *([qkv] * 5))


def _swa_sample(qkv, sink, ck, cv, o_buf, *, row0, nb, seq, hq, hk, hd):
    qd, kd = hq * hd, hk * hd
    win = ck.shape[1]
    rb0 = row0 // seq
    grid_spec = pltpu.PrefetchScalarGridSpec(
        num_scalar_prefetch=1,
        grid=(nb,),
        in_specs=[
            pl.BlockSpec((seq, qd), lambda b, s: (rb0 + b, 0)),
            pl.BlockSpec((seq, kd), lambda b, s: (rb0 + b, qd // kd)),
            pl.BlockSpec((seq, kd), lambda b, s: (rb0 + b, qd // kd + 1)),
            pl.BlockSpec((1, win, kd), lambda b, s: (b, 0, 0)),
            pl.BlockSpec((1, win, kd), lambda b, s: (b, 0, 0)),
            _ANY,
        ],
        out_specs=pl.BlockSpec((seq, qd), lambda b, s: (rb0 + b, 0)),
    )
    return pl.pallas_call(
        functools.partial(_swa_sample_kernel, hk=hk, grp=hq // hk, hd=hd),
        grid_spec=grid_spec,
        out_shape=jax.ShapeDtypeStruct(o_buf.shape, o_buf.dtype),
        input_output_aliases={6: 0},
        compiler_params=_cparams(("parallel",)),
        name="swa_sample",
    )(sink, qkv, qkv, qkv, ck.reshape(nb, win, kd), cv.reshape(nb, win, kd), o_buf)


def _pool_kernel(*refs, has_hist, n_alias, pos0):
    x_ref, g_ref = refs[:2]
    k0 = 2
    if has_hist:
        hist_ref = refs[k0]
        k0 += 1
    w_ref, sc_ref = refs[k0], refs[k0 + 1]
    o_ref, st_ref, hb = refs[k0 + 2 + n_alias:]
    l = pl.program_id(1)
    tm, d = x_ref.shape
    gd = w_ref.shape[1]

    @pl.when(l == 0)
    def _():
        if has_hist:
            hb[0:HIST_ROWS - POOL_STATE, :] = jnp.zeros((HIST_ROWS - POOL_STATE, d), F32)
            hb[HIST_ROWS - POOL_STATE:HIST_ROWS, :] = hist_ref[0]
        else:
            hb[0:HIST_ROWS, :] = jnp.zeros((HIST_ROWS, d), F32)

    x = x_ref[...]
    hb[HIST_ROWS:HIST_ROWS + tm, :] = _rms(x, g_ref[...])
    pos = pos0 + l * tm + lax.broadcasted_iota(jnp.int32, (tm, 1), 0)
    for gi, w in enumerate(POOL_WINDOWS):
        cs = slice(gi * gd, (gi + 1) * gd)
        hcur = hb[HIST_ROWS:HIST_ROWS + tm, cs]
        acc = hcur
        for j in range(1, w):
            acc = acc + hb[HIST_ROWS - j:HIST_ROWS - j + tm, cs]
        cnt = jnp.minimum(w, pos + 1).astype(F32)
        pooled = acc / cnt - hcur
        mixed = _dot(pooled.astype(BF16), w_ref[gi])
        o_ref[:, cs] = x[:, cs] + mixed * sc_ref[:, cs]
    tail = hb[tm + HIST_ROWS - POOL_STATE:tm + HIST_ROWS, :]
    hb[HIST_ROWS - POOL_STATE:HIST_ROWS, :] = tail

    @pl.when(l == pl.num_programs(1) - 1)
    def _():
        st_ref[0] = tail


def _pool(x, g, hist, w, scale, *, row0, nb, seq, pos0, o_buf=None):
    t, d = x.shape
    ng, gd = w.shape[0], w.shape[1]
    tm = _pick(seq, (512, 256, 128, 64, 32))
    nl = seq // tm
    rb0 = row0 // tm
    assert row0 % tm == 0 and ng == len(POOL_WINDOWS) and tm >= POOL_STATE
    rows = lambda b, l: (rb0 + b * nl + l, 0)
    in_specs = [pl.BlockSpec((tm, d), rows), pl.BlockSpec((1, d), lambda b, l: (0, 0))]
    args = [x, g.reshape(1, d)]
    if hist is not None:
        in_specs.append(pl.BlockSpec((1, POOL_STATE, d), lambda b, l: (b, 0, 0)))
        args.append(hist)
    in_specs += [pl.BlockSpec((ng, gd, gd), lambda b, l: (0, 0, 0)), pl.BlockSpec((1, d), lambda b, l: (0, 0))]
    args += [w, scale.reshape(1, d)]
    aliases = {}
    if o_buf is not None:
        aliases[len(args)] = 0
        in_specs.append(_ANY)
        args.append(o_buf)
    return pl.pallas_call(
        functools.partial(_pool_kernel, has_hist=hist is not None, n_alias=len(aliases), pos0=pos0),
        grid=(nb, nl),
        in_specs=in_specs,
        out_specs=[pl.BlockSpec((tm, d), rows), pl.BlockSpec((1, POOL_STATE, d), lambda b, l: (b, 0, 0))],
        out_shape=[jax.ShapeDtypeStruct((t, d), F32), jax.ShapeDtypeStruct((nb, POOL_STATE, d), F32)],
        scratch_shapes=[pltpu.VMEM((HIST_ROWS + tm, d), F32)],
        input_output_aliases=aliases,
        compiler_params=_cparams(("parallel", "arbitrary")),
        name="pool_mixer",
    )(*args)


def _router_kernel(x_ref, g_ref, whi_ref, wlo_ref, b_ref, h_ref, info_ref, cnt_ref, carry, *, ng, epg):
    i = pl.program_id(0)
    tm = x_ref.shape[0]

    @pl.when(i == 0)
    def _():
        carry[...] = jnp.zeros_like(carry)

    h = _rms(x_ref[...], g_ref[...])
    h_ref[...] = h
    h_hi = h.astype(BF16)
    h_lo = (h - h_hi.astype(F32)).astype(BF16)
    logits = (_dot(h_hi, whi_ref[...]) + (_dot(h_hi, wlo_ref[...]) + _dot(h_lo, whi_ref[...]))) + b_ref[...]
    lane = lax.broadcasted_iota(jnp.int32, logits.shape, 1)
    ninf = -jnp.inf
    big = jnp.int32(LANES)
    is_g = lane < ng
    gl = jnp.where(is_g, logits, ninf)
    gmax = jnp.max(gl, axis=-1, keepdims=True)
    gidx = jnp.min(jnp.where(gl == gmax, lane, big), axis=-1, keepdims=True)
    g_w = 1.0 / jnp.sum(jnp.where(is_g, jnp.exp(gl - gmax), 0.0), axis=-1, keepdims=True)
    lo = ng + gidx * epg
    el = jnp.where((lane >= lo) & (lane < lo + epg), logits, ninf)
    v1 = jnp.max(el, axis=-1, keepdims=True)
    i1 = jnp.min(jnp.where(el == v1, lane, big), axis=-1, keepdims=True)
    el2 = jnp.where(lane == i1, ninf, el)
    v2 = jnp.max(el2, axis=-1, keepdims=True)
    i2 = jnp.min(jnp.where(el2 == v2, lane, big), axis=-1, keepdims=True)
    e = jnp.exp(v2 - v1)
    w1 = (1.0 / (1.0 + e)) * g_w
    w2 = (e / (1.0 + e)) * g_w
    e1, e2 = i1 - ng, i2 - ng
    hot = (lane == e1) | (lane == e2)
    row = lax.broadcasted_iota(jnp.int32, (tm, tm), 0)
    col = lax.broadcasted_iota(jnp.int32, (tm, tm), 1)
    before = _dot((col < row).astype(BF16), hot.astype(BF16)) + carry[...]
    r1 = jnp.sum(jnp.where(lane == e1, before, 0.0), axis=-1, keepdims=True)
    r2 = jnp.sum(jnp.where(lane == e2, before, 0.0), axis=-1, keepdims=True)
    fields = (e1.astype(F32), e2.astype(F32), w1, w2, r1, r2)
    info = jnp.zeros(logits.shape, F32)
    for n, f in enumerate(fields):
        info = jnp.where(lane == n, f, info)
    info_ref[...] = info
    total = carry[...] + jnp.sum(hot.astype(F32), axis=0, keepdims=True)
    carry[...] = total
    cnt_ref[...] = total


def _router(x, g, w_cat, b_cat, *, ng, epg):
    t, d = x.shape
    tm = _pick(t, (256, 128))
    w_hi = w_cat.astype(BF16)
    w_lo = (w_cat - w_hi.astype(F32)).astype(BF16)
    return pl.pallas_call(
        functools.partial(_router_kernel, ng=ng, epg=epg),
        grid=(t // tm,),
        in_specs=[
            pl.BlockSpec((tm, d), lambda i: (i, 0)),
            pl.BlockSpec((1, d), lambda i: (0, 0)),
            pl.BlockSpec((d, LANES), lambda i: (0, 0)),
            pl.BlockSpec((d, LANES), lambda i: (0, 0)),
            pl.BlockSpec((1, LANES), lambda i: (0, 0)),
        ],
        out_specs=[
            pl.BlockSpec((tm, d), lambda i: (i, 0)),
            pl.BlockSpec((tm, LANES), lambda i: (i, 0)),
            pl.BlockSpec((1, LANES), lambda i: (0, 0)),
        ],
        out_shape=[jax.ShapeDtypeStruct((t, d), F32), jax.ShapeDtypeStruct((t, LANES), F32),
                   jax.ShapeDtypeStruct((1, LANES), F32)],
        scratch_shapes=[pltpu.VMEM((1, LANES), F32)],
        compiler_params=_cparams(("arbitrary",)),
        name="moe_router",
    )(x, g.reshape(1, d), w_hi, w_lo, b_cat)


def _row_copy(src_hbm, row, dst, r, sem):
    return pltpu.make_async_copy(src_hbm.at[pl.ds(row, 1), :], dst.at[pl.ds(r, 1), :], sem)


def _gather_rows(src_hbm, idx_ref, base, dst, sem, n, start):
    if not start:
        pltpu.make_async_copy(src_hbm.at[pl.ds(0, n), :], dst, sem).wait()
        return
    for r in range(n):
        _row_copy(src_hbm, idx_ref[base + r], dst, r, sem).start()


def _experts_kernel(te_ref, src_ref, nu_ref, h_hbm, wg_ref, wu_ref, wd_ref, y_ref, buf, wgb, wub, wdb, sem):
    i = pl.program_id(0)
    tm = y_ref.shape[0]
    n_used = nu_ref[0]

    @pl.when((i == 0) & (n_used > 0))
    def _():
        _gather_rows(h_hbm, src_ref, 0, buf.at[0], sem.at[0], tm, True)

    @pl.when(i + 1 < n_used)
    def _():
        nxt = (i + 1) % 2
        _gather_rows(h_hbm, src_ref, (i + 1) * tm, buf.at[nxt], sem.at[nxt], tm, True)

    @pl.when(i < n_used)
    def _():
        @pl.when((i == 0) | (te_ref[i] != te_ref[jnp.maximum(i - 1, 0)]))
        def _():
            wgb[...] = wg_ref[...].astype(BF16)
            wub[...] = wu_ref[...].astype(BF16)
            wdb[...] = wd_ref[...].astype(BF16)

        slot = i % 2
        _gather_rows(h_hbm, src_ref, i * tm, buf.at[slot], sem.at[slot], tm, False)
        xb = buf[slot].astype(BF16)
        gt = _dot(xb, wgb[...])
        up = _dot(xb, wub[...])
        a = (gt * _sigmoid(gt)) * up
        y_ref[...] = _dot(a.astype(BF16), wdb[...])

    @pl.when(i >= n_used)
    def _():
        y_ref[...] = jnp.zeros_like(y_ref)


def _experts(h, tile_expert, src, n_used, wg, wu, wd, layer):
    d = h.shape[1]
    ff = wg.shape[3]
    tm = EXPERT_TILE
    p = src.shape[0]
    grid_spec = pltpu.PrefetchScalarGridSpec(
        num_scalar_prefetch=3,
        grid=(p // tm,),
        in_specs=[
            _ANY,
            pl.BlockSpec((None, None, d, ff), lambda i, te, s, nu: (layer, te[i], 0, 0)),
            pl.BlockSpec((None, None, d, ff), lambda i, te, s, nu: (layer, te[i], 0, 0)),
            pl.BlockSpec((None, None, ff, d), lambda i, te, s, nu: (layer, te[i], 0, 0)),
        ],
        out_specs=pl.BlockSpec((tm, d), lambda i, te, s, nu: (i, 0)),
        scratch_shapes=[pltpu.VMEM((2, tm, d), F32), pltpu.VMEM((d, ff), BF16), pltpu.VMEM((d, ff), BF16),
                        pltpu.VMEM((ff, d), BF16), pltpu.SemaphoreType.DMA((2,))],
    )
    return pl.pallas_call(
        _experts_kernel,
        grid_spec=grid_spec,
        out_shape=jax.ShapeDtypeStruct((p, d), F32),
        compiler_params=_cparams(("arbitrary",)),
        name="moe_experts",
    )(tile_expert, src, n_used, h, wg, wu, wd)


def _ple_kernel(p0_ref, p1_ref, x_ref, info_ref, p_ref, y_hbm, g_ref, wup_ref, wgt_ref, gn_ref,
                xo_ref, hn_ref, ybuf, sem):
    i = pl.program_id(0)
    n = pl.num_programs(0)
    tm = x_ref.shape[0]

    def gather(tile, slot, start):
        for k, pref in enumerate((p0_ref, p1_ref)):
            _gather_rows(y_hbm, pref, tile * tm, ybuf.at[slot, k], sem.at[slot], tm, start)

    @pl.when(i == 0)
    def _():
        gather(0, 0, True)

    @pl.when(i + 1 < n)
    def _():
        gather(i + 1, (i + 1) % 2, True)

    slot = i % 2
    gather(i, slot, False)
    info = info_ref[...]
    x2 = x_ref[...] + info[:, 2:3] * ybuf[slot, 0] + info[:, 3:4] * ybuf[slot, 1]
    h3 = _rms(x2, g_ref[...]).astype(BF16)
    pe = _dot(p_ref[...].astype(BF16), wup_ref[...])
    x3 = x2 + pe * _sigmoid(_dot(h3, wgt_ref[...]))
    xo_ref[...] = x3
    hn_ref[...] = _rms(x3, gn_ref[...]).astype(hn_ref.dtype)


def _combine_ple(x, info, p, y, pos0, pos1, g, wup, wgt, gn, hn_dtype):
    t, d = x.shape
    pd = p.shape[1]
    tm = _pick(t, (256, 128))
    const = lambda i, a, b: (0, 0)
    grid_spec = pltpu.PrefetchScalarGridSpec(
        num_scalar_prefetch=2,
        grid=(t // tm,),
        in_specs=[
            pl.BlockSpec((tm, d), lambda i, a, b: (i, 0)),
            pl.BlockSpec((tm, LANES), lambda i, a, b: (i, 0)),
            pl.BlockSpec((tm, pd), lambda i, a, b: (i, 0)),
            _ANY,
            pl.BlockSpec((1, d), const),
            pl.BlockSpec((pd, d), const),
            pl.BlockSpec((d, d), const),
            pl.BlockSpec((1, d), const),
        ],
        out_specs=[pl.BlockSpec((tm, d), lambda i, a, b: (i, 0)), pl.BlockSpec((tm, d), lambda i, a, b: (i, 0))],
        scratch_shapes=[pltpu.VMEM((2, 2, tm, d), F32), pltpu.SemaphoreType.DMA((2,))],
    )
    return pl.pallas_call(
        _ple_kernel,
        grid_spec=grid_spec,
        out_shape=[jax.ShapeDtypeStruct((t, d), F32), jax.ShapeDtypeStruct((t, d), hn_dtype)],
        compiler_params=_cparams(("arbitrary",)),
        name="moe_combine_ple",
    )(pos0, pos1, x, info, p, y, g.reshape(1, d), wup, wgt, gn.reshape(1, d))


def _moe_ple(x1, p, layer, g_ffn, w_group, b_group, w_router, b_router, w_gate, w_up, w_down, g_ple, ple_up, ple_gate,
             g_next, hn_dtype):
    t, d = x1.shape
    ng = w_group.shape[1]
    ne = w_router.shape[1]
    epg = ne // ng
    assert ng + ne <= LANES
    w_cat = jnp.zeros((d, LANES), F32).at[:, :ng].set(w_group).at[:, ng:ng + ne].set(w_router)
    b_cat = jnp.zeros((1, LANES), F32).at[0, :ng].set(b_group).at[0, ng:ng + ne].set(b_router)
    h2, info, cnt = _router(x1, g_ffn, w_cat, b_cat, ng=ng, epg=epg)

    tm = EXPERT_TILE
    n_tiles = (2 * t) // tm + ne
    counts = cnt[0, :ne].astype(jnp.int32)
    padded = ((counts + tm - 1) // tm) * tm
    ends = jnp.cumsum(padded)
    offs = ends - padded
    eid = info[:, 0:2].astype(jnp.int32)
    rank = info[:, 4:6].astype(jnp.int32)
    pos = offs[eid] + rank
    n_used = (ends[-1] // tm).astype(jnp.int32).reshape(1)
    tile_start = jnp.arange(n_tiles, dtype=jnp.int32) * tm
    tile_expert = jnp.minimum(jnp.sum((tile_start[:, None] >= ends[None, :]).astype(jnp.int32), axis=1), ne - 1)
    tok = jnp.broadcast_to(jnp.arange(t, dtype=jnp.int32)[:, None], (t, 2))
    src = jnp.zeros((n_tiles * tm,), jnp.int32).at[pos.reshape(-1)].set(tok.reshape(-1))

    y = _experts(h2, tile_expert, src, n_used, w_gate, w_up, w_down, layer)
    return _combine_ple(x1, info, p, y, pos[:, 0], pos[:, 1], g_ple, ple_up, ple_gate, g_next, hn_dtype)


def kernel(x_prompt, x_sample, state_ret, cache_swa_k, cache_swa_v, state_pool, p_prompt, p_sample, norm_mix, norm_ffn, norm_ple, norm_final, ret_w_in, ret_gn, ret_w_out, swa_w_in, swa_sink, swa_w_out, pool_w, pool_scale, moe_w_group, moe_b_group, moe_w_router, moe_b_router, moe_w_gate, moe_w_up, moe_w_down, ple_w_up, ple_w_gate):
    bp, lp, d = x_prompt.shape
    bs, ls, _ = x_sample.shape
    tp, ts = bp * lp, bs * ls
    depth = norm_mix.shape[0]
    pd = p_prompt.shape[-1]
    n_ret, _, nh, dk, dv = state_ret.shape
    _, _, win, hk, hd = cache_swa_k.shape
    hq = swa_sink.shape[1]
    assert ls <= CHUNK and ls <= win and lp % CHUNK == 0 and lp >= win and win % CHUNK == 0

    x = jnp.concatenate([x_prompt.reshape(tp, d), x_sample.reshape(ts, d)], axis=0)
    pos = jnp.concatenate([jnp.tile(jnp.arange(lp, dtype=jnp.int32), bp),
                           jnp.tile(PAST_LEN + jnp.arange(ls, dtype=jnp.int32), bs)])
    lg = jnp.log1p(-jnp.exp2(-5.0 - jnp.arange(nh, dtype=F32)))
    ret_cl = _pick(lp, (256, 128, 64))
    tn_d = _pick(d, (512, 256, 128))
    ple_up, ple_gate = ple_w_up.astype(BF16), ple_w_gate.astype(BF16)
    pool_wb = pool_w.astype(BF16)

    hn = _norm(x, norm_mix[0], BF16)
    ret_p = ret_s = None
    k_p, k_s, v_p, v_s, pool_p, pool_s = [], [], [], [], [], []
    for i in range(depth):
        kind, j = i % 3, i // 3
        if kind == 0:
            qkw = 2 * nh * dk + nh * dv
            qkv = _matmul(hn, ret_w_in, j, tn=tn_d, ncols=qkw, rope=(dk, 2 * nh * dk // tn_d, nh * dk // tn_d, dk ** -0.5),
                          tables=_rope_tables(pos, dk), out_dtype=BF16, name="ret_in_proj")
            gate = _matmul(hn, ret_w_in, j, tn=tn_d, col0=qkw, name="ret_gate_proj")
            o, ret_p = _retention(qkv, gate, None, ret_gn[j], lg, layer=j, n_layers=n_ret, row0=0, nb=bp, seq=lp,
                                  cl=ret_cl, st_buf=ret_p)
            o, ret_s = _retention(qkv, gate, state_ret, ret_gn[j], lg, layer=j, n_layers=n_ret, row0=tp, nb=bs, seq=ls,
                                  cl=ls, o_buf=o, st_buf=ret_s)
            x1 = _matmul(o, ret_w_out, j, tn=tn_d, res=x, name="ret_out_proj")
        elif kind == 1:
            qd, kd = hq * hd, hk * hd
            tn = _pick(kd, (512, 256, 128))
            qkv = _matmul(hn, swa_w_in, j, tn=tn, rope=(hd, (qd + kd) // tn, 0, 1.0),
                          tables=_rope_tables(pos, hd), name="swa_in_proj")
            o = _swa_prompt(qkv, swa_sink[j], nb=bp, seq=lp, win=win, hq=hq, hk=hk, hd=hd)
            o = _swa_sample(qkv, swa_sink[j], cache_swa_k[j], cache_swa_v[j], o, row0=tp, nb=bs, seq=ls,
                            hq=hq, hk=hk, hd=hd)
            kv_p = jnp.stack([lax.slice(qkv, (b * lp + lp - win, qd), ((b + 1) * lp, qd + 2 * kd)) for b in range(bp)])
            kv_s = lax.slice(qkv, (tp, qd), (tp + ts, qd + 2 * kd)).reshape(bs, ls, 2 * kd)
            k_p.append(kv_p[..., :kd].reshape(bp, win, hk, hd))
            v_p.append(kv_p[..., kd:].reshape(bp, win, hk, hd))
            k_s.append(jnp.concatenate([cache_swa_k[j][:, ls:], kv_s[..., :kd].reshape(bs, ls, hk, hd)], axis=1))
            v_s.append(jnp.concatenate([cache_swa_v[j][:, ls:], kv_s[..., kd:].reshape(bs, ls, hk, hd)], axis=1))
            x1 = _matmul(o, swa_w_out, j, tn=tn_d, res=x, name="swa_out_proj")
        else:
            x1, st_p = _pool(x, norm_mix[i], None, pool_wb[j], pool_scale[j], row0=0, nb=bp, seq=lp, pos0=0)
            x1, st_s = _pool(x, norm_mix[i], state_pool[j], pool_wb[j], pool_scale[j], row0=tp, nb=bs, seq=ls,
                             pos0=PAST_LEN, o_buf=x1)
            pool_p.append(st_p)
            pool_s.append(st_s)
        p = jnp.concatenate([p_prompt[i].reshape(tp, pd), p_sample[i].reshape(ts, pd)], axis=0)
        last = i == depth - 1
        g_next = norm_final if last else norm_mix[i + 1]
        x, hn = _moe_ple(x1, p, i, norm_ffn[i], moe_w_group[i], moe_b_group[i], moe_w_router[i], moe_b_router[i],
                         moe_w_gate, moe_w_up, moe_w_down, norm_ple[i], ple_up[i], ple_gate[i],
                         g_next, F32 if last else BF16)
    y = hn
    return (y[:tp].reshape(bp, lp, d), y[tp:].reshape(bs, ls, d), ret_p, ret_s,
            jnp.stack(k_p), jnp.stack(k_s), jnp.stack(v_p), jnp.stack(v_s), jnp.stack(pool_p), jnp.stack(pool_s))
```

```python
import functools

import jax
import jax.numpy as jnp
from jax import lax
from jax.experimental import pallas as pl
from jax.experimental.pallas import tpu as pltpu

F32 = jnp.float32
BF16 = jnp.bfloat16
EPS = 1e-6
ROPE_THETA = 10000.0
PAST_LEN = 4096
CHUNK = 64
POOL_WINDOWS = (2, 4, 8, 16)
POOL_STATE = POOL_WINDOWS[-1] - 1
HIST_ROWS = 16
VMEM_LIMIT = 56 * 1024 * 1024
EXPERT_TILE = 256
LANES = 128
MM_SUB = 256
GATHER_DMA_PRIORITY = 1
PLE_GROUPS = 8


def _pick(n, cands):
    for c in cands:
        if n % c == 0:
            return c
    raise ValueError(f"no tile in {cands} divides {n}")


def _cparams(sem):
    return pltpu.CompilerParams(dimension_semantics=sem, vmem_limit_bytes=VMEM_LIMIT)


def _rms(x, g):
    return x * lax.rsqrt(jnp.mean(x * x, axis=-1, keepdims=True) + EPS) * g


def _dot(a, b):
    return jnp.dot(a, b, preferred_element_type=F32)


def _dot_nt(a, b):
    return lax.dot_general(a, b, (((1,), (1,)), ((), ())), preferred_element_type=F32)


def _dot_tn(a, b):
    return lax.dot_general(a, b, (((0,), (0,)), ((), ())), preferred_element_type=F32)


def _sigmoid(x):
    return 1.0 / (1.0 + jnp.exp(-x))


_ANY = pl.BlockSpec(memory_space=pl.ANY)


def _norm_kernel(x_ref, g_ref, o_ref):
    o_ref[...] = _rms(x_ref[...], g_ref[...]).astype(o_ref.dtype)


def _norm(x, g, out_dtype):
    t, d = x.shape
    tm = _pick(t, (512, 256, 128))
    return pl.pallas_call(
        _norm_kernel,
        grid=(t // tm,),
        in_specs=[pl.BlockSpec((tm, d), lambda i: (i, 0)), pl.BlockSpec((1, d), lambda i: (0, 0))],
        out_specs=pl.BlockSpec((tm, d), lambda i: (i, 0)),
        out_shape=jax.ShapeDtypeStruct((t, d), out_dtype),
        compiler_params=_cparams(("parallel",)),
        name="rmsnorm",
    )(x, g.reshape(1, d))


def _mm_kernel(*refs, has_res, rope):
    a_ref, w_ref = refs[0], refs[1]
    k = 2
    if has_res:
        r_ref = refs[k]
        k += 1
    if rope is not None:
        cos_ref, sin_ref = refs[k], refs[k + 1]
        k += 2
    o_ref = refs[k]
    tn = o_ref.shape[1]
    sub = min(tn, MM_SUB)
    a = a_ref[...]
    j = pl.program_id(1)
    for c0 in range(0, tn, sub):
        cs = slice(c0, c0 + sub)
        acc = _dot(a, w_ref[:, cs].astype(BF16))
        if has_res:
            acc = acc + r_ref[:, cs]
        if rope is not None:
            hd, rope_cols, q_cols, q_scale = rope
            half = hd // 2
            col = j * tn + c0
            rep = sub // cos_ref.shape[1]
            c, s = cos_ref[...], sin_ref[...]
            if rep > 1:
                c, s = jnp.tile(c, (1, rep)), jnp.tile(s, (1, rep))
            lane = lax.broadcasted_iota(jnp.int32, acc.shape, 1)
            partner = jnp.where((lane % hd) < half, pltpu.roll(acc, sub - half, 1), pltpu.roll(acc, half, 1))
            y = acc * c + partner * s
            if q_scale != 1.0:
                y = y * jnp.where(col < q_cols, q_scale, 1.0)
            acc = jnp.where(col < rope_cols, y, acc)
        o_ref[:, cs] = acc.astype(o_ref.dtype)


def _matmul(a, w, layer, *, tn, col0=0, ncols=None, res=None, rope=None, tables=None, out_dtype=F32, name="matmul"):
    m, kd = a.shape
    n = w.shape[2] - col0 if ncols is None else ncols
    assert col0 % tn == 0 and n % tn == 0
    cb0 = col0 // tn
    tm = _pick(m, (1024, 512, 256, 128))
    in_specs = [pl.BlockSpec((tm, kd), lambda i, j: (i, 0)),
                pl.BlockSpec((None, kd, tn), lambda i, j: (layer, 0, cb0 + j))]
    args = [a, w]
    if res is not None:
        in_specs.append(pl.BlockSpec((tm, tn), lambda i, j: (i, j)))
        args.append(res)
    if rope is not None:
        tw = tables[0].shape[1]
        in_specs += [pl.BlockSpec((tm, tw), lambda i, j: (i, 0))] * 2
        args += list(tables)
    return pl.pallas_call(
        functools.partial(_mm_kernel, has_res=res is not None, rope=rope),
        grid=(m // tm, n // tn),
        in_specs=in_specs,
        out_specs=pl.BlockSpec((tm, tn), lambda i, j: (i, j)),
        out_shape=jax.ShapeDtypeStruct((m, n), out_dtype),
        compiler_params=_cparams(("parallel", "arbitrary")),
        name=name,
    )(*args)


def _rope_tables(pos, hd):
    half = hd // 2
    inv = 1.0 / jnp.power(ROPE_THETA, jnp.arange(half, dtype=F32) / half)
    ang = pos.astype(F32)[:, None] * inv[None, :]
    c, s = jnp.cos(ang), jnp.sin(ang)
    cosf = jnp.concatenate([c, c], axis=-1)
    sinf = jnp.concatenate([-s, s], axis=-1)
    rep = max(LANES // hd, 1)
    return jnp.tile(cosf, (1, rep)), jnp.tile(sinf, (1, rep))


def _ret_kernel(*refs, has_s0, n_alias):
    lg_ref, q_ref, k_ref, v_ref, g_ref = refs[:5]
    k0 = 5
    if has_s0:
        s0_ref = refs[k0]
        k0 += 1
    gn_ref = refs[k0]
    o_ref, sn_ref, s_scr, dec_scr = refs[k0 + 1 + n_alias:]
    hblk, c = pl.program_id(1), pl.program_id(2)
    cl = q_ref.shape[0]
    hb, dk, dv = s_scr.shape
    idx = lax.broadcasted_iota(jnp.int32, (cl, 1), 0).astype(F32)
    for hh in range(hb):
        lg = lg_ref[hblk * hb + hh]
        ks, vs = slice(hh * dk, (hh + 1) * dk), slice(hh * dv, (hh + 1) * dv)

        @pl.when(c == 0)
        def _():
            s_scr[hh] = s0_ref[hh] if has_s0 else jnp.zeros((dk, dv), F32)
            row = lax.broadcasted_iota(jnp.int32, (cl, cl), 0)
            col = lax.broadcasted_iota(jnp.int32, (cl, cl), 1)
            diff = (row - col).astype(F32)
            dec_scr[hh] = jnp.where(diff >= 0, jnp.exp(jnp.maximum(diff, 0.0) * lg), 0.0)

    for hh in range(hb):
        lg = lg_ref[hblk * hb + hh]
        ks, vs = slice(hh * dk, (hh + 1) * dk), slice(hh * dv, (hh + 1) * dv)
        qb, kb, vb = q_ref[:, ks], k_ref[:, ks], v_ref[:, vs]
        scores = _dot_nt(qb, kb) * dec_scr[hh]
        inner = _dot(scores.astype(BF16), vb)
        s_prev = s_scr[hh]
        cross = _dot(qb, s_prev.astype(BF16)) * jnp.exp((idx + 1.0) * lg)
        k_dec = kb.astype(F32) * jnp.exp((cl - 1.0 - idx) * lg)
        chunk_decay = jnp.exp(jnp.full((1, 1), float(cl), F32) * lg)
        s_new = s_prev * chunk_decay + _dot_tn(k_dec.astype(BF16), vb)
        s_scr[hh] = s_new
        o = inner + cross
        mu = jnp.mean(o, axis=-1, keepdims=True)
        var = jnp.mean(jnp.square(o - mu), axis=-1, keepdims=True)
        on = (o - mu) * lax.rsqrt(var + EPS) * gn_ref[:, vs]
        g = g_ref[:, vs]
        o_ref[:, vs] = ((g * _sigmoid(g)) * on).astype(o_ref.dtype)

    @pl.when(c == pl.num_programs(2) - 1)
    def _():
        sn_ref[...] = s_scr[...]


def _retention(qkv, gate, s0_all, gn, lg, *, layer, n_layers, row0, nb, seq, cl, hb, o_buf=None, st_buf=None):
    t = qkv.shape[0]
    nh = lg.shape[0]
    dv = gate.shape[1] // nh
    dk = (qkv.shape[1] - nh * dv) // (2 * nh)
    assert dv == 2 * dk and row0 % cl == 0 and seq % cl == 0 and nh % hb == 0
    nc = seq // cl
    rb0 = row0 // cl
    nhb = nh // hb
    rowmap = lambda b, c: rb0 + b * nc + c
    st_spec = pl.BlockSpec((None, None, hb, dk, dv), lambda b, h, c, lg: (layer, b, h, 0, 0))
    in_specs = [
        pl.BlockSpec((cl, hb * dk), lambda b, h, c, lg: (rowmap(b, c), h)),
        pl.BlockSpec((cl, hb * dk), lambda b, h, c, lg: (rowmap(b, c), nhb + h)),
        pl.BlockSpec((cl, hb * dv), lambda b, h, c, lg: (rowmap(b, c), nhb + h)),
        pl.BlockSpec((cl, hb * dv), lambda b, h, c, lg: (rowmap(b, c), h)),
    ]
    args = [lg, qkv, qkv, qkv, gate]
    if s0_all is not None:
        in_specs.append(st_spec)
        args.append(s0_all)
    in_specs.append(pl.BlockSpec((1, hb * dv), lambda b, h, c, lg: (0, h)))
    args.append(gn.reshape(1, nh * dv))
    aliases = {}
    for out_idx, buf in enumerate((o_buf, st_buf)):
        if buf is not None:
            aliases[len(args)] = out_idx
            in_specs.append(_ANY)
            args.append(buf)
    grid_spec = pltpu.PrefetchScalarGridSpec(
        num_scalar_prefetch=1,
        grid=(nb, nhb, nc),
        in_specs=in_specs,
        out_specs=[pl.BlockSpec((cl, hb * dv), lambda b, h, c, lg: (rowmap(b, c), h)), st_spec],
        scratch_shapes=[pltpu.VMEM((hb, dk, dv), F32), pltpu.VMEM((hb, cl, cl), F32)],
    )
    return pl.pallas_call(
        functools.partial(_ret_kernel, has_s0=s0_all is not None, n_alias=len(aliases)),
        grid_spec=grid_spec,
        out_shape=[jax.ShapeDtypeStruct((t, nh * dv), BF16), jax.ShapeDtypeStruct((n_layers, nb, nh, dk, dv), F32)],
        input_output_aliases=aliases,
        compiler_params=_cparams(("parallel", "parallel", "arbitrary")),
        name="retention",
    )(*args)


def _attend(sink_ref, q_ref, k_all, v_all, bias, o_ref, *, hk, grp, hd):
    assert 2 * hd == LANES and grp % 2 == 0 and hk % 2 == 0
    lq = q_ref.shape[0]
    s_len = k_all.shape[0]
    nhq = hk * grp
    pair_rows = 2 * grp * lq
    lane = lax.broadcasted_iota(jnp.int32, (lq, LANES), 1)
    own = [(lane // hd) == a for a in range(2)]
    s_parts = []
    for kp in range(hk // 2):
        rows = []
        for hq in range(kp * 2 * grp, (kp + 1) * 2 * grp):
            a = (hq // grp) % 2
            blk = q_ref[:, (hq // 2) * LANES:(hq // 2 + 1) * LANES]
            if hq % 2 != a:
                blk = pltpu.roll(blk, hd, 1)
            rows.append(jnp.where(own[a], blk, 0.0))
        qs = jnp.concatenate(rows, axis=0).astype(BF16)
        s_parts.append(_dot_nt(qs, k_all[:, kp * LANES:(kp + 1) * LANES]))
    s = jnp.concatenate(s_parts, axis=0) * (hd ** -0.5)
    if bias is not None:
        s = (s.reshape(nhq, lq, s_len) + bias[None]).reshape(nhq * lq, s_len)
    sink = jnp.concatenate([jnp.full((lq, 1), sink_ref[hq], F32) for hq in range(nhq)], axis=0)
    m = jnp.maximum(jnp.max(s, axis=-1, keepdims=True), sink)
    e = jnp.exp(s - m)
    pr = (e / (jnp.sum(e, axis=-1, keepdims=True) + jnp.exp(sink - m))).astype(BF16)
    for kp in range(hk // 2):
        oh = _dot(pr[kp * pair_rows:(kp + 1) * pair_rows], v_all[:, kp * LANES:(kp + 1) * LANES])
        for hq in range(kp * 2 * grp, (kp + 1) * 2 * grp, 2):
            a = (hq // grp) % 2
            r0 = (hq - kp * 2 * grp) * lq
            lo, hi = oh[r0:r0 + lq], oh[r0 + lq:r0 + 2 * lq]
            if a == 1:
                lo = pltpu.roll(lo, hd, 1)
            else:
                hi = pltpu.roll(hi, hd, 1)
            o_ref[:, (hq // 2) * LANES:(hq // 2 + 1) * LANES] = jnp.where(lane < hd, lo, hi).astype(o_ref.dtype)


def _swa_prompt_kernel(sink_ref, q_ref, kp_ref, kc_ref, vp_ref, vc_ref, o_ref, *, hk, grp, hd, win_chunks):
    c = pl.program_id(1)
    lq = q_ref.shape[0]
    nqb = lq // CHUNK
    k_all = jnp.concatenate([kp_ref[...], kc_ref[...]], axis=0).astype(BF16)
    v_all = jnp.concatenate([vp_ref[...], vc_ref[...]], axis=0).astype(BF16)
    qc = lax.broadcasted_iota(jnp.int32, (lq, 2 * lq), 0) // CHUNK
    kc = lax.broadcasted_iota(jnp.int32, (lq, 2 * lq), 1) // CHUNK
    back = qc + nqb - kc
    valid = (back >= 0) & (back <= win_chunks) & ((c - 1) * nqb + kc >= 0)
    bias = jnp.where(valid, 0.0, -jnp.inf).astype(F32)
    _attend(sink_ref, q_ref, k_all, v_all, bias, o_ref, hk=hk, grp=grp, hd=hd)


def _swa_sample_kernel(sink_ref, q_ref, kn_ref, vn_ref, kc_ref, vc_ref, o_alias, o_ref, *, hk, grp, hd):
    del o_alias
    k_all = jnp.concatenate([kc_ref[0], kn_ref[...]], axis=0).astype(BF16)
    v_all = jnp.concatenate([vc_ref[0], vn_ref[...]], axis=0).astype(BF16)
    _attend(sink_ref, q_ref, k_all, v_all, None, o_ref, hk=hk, grp=grp, hd=hd)


def _swa_prompt(qkv, sink, *, nb, seq, win, hq, hk, hd):
    t = qkv.shape[0]
    qd, kd = hq * hd, hk * hd
    win_chunks = win // CHUNK
    lq = _pick(seq, (2 * CHUNK, CHUNK))
    assert qd % kd == 0 and lq // CHUNK >= win_chunks
    nblk = seq // lq
    kcol, vcol = qd // kd, qd // kd + 1
    cur = lambda col: (lambda b, c, s: (b * nblk + c, col))
    prev = lambda col: (lambda b, c, s: (b * nblk + jnp.maximum(c - 1, 0), col))
    grid_spec = pltpu.PrefetchScalarGridSpec(
        num_scalar_prefetch=1,
        grid=(nb, nblk),
        in_specs=[pl.BlockSpec((lq, qd), cur(0)),
                  pl.BlockSpec((lq, kd), prev(kcol)), pl.BlockSpec((lq, kd), cur(kcol)),
                  pl.BlockSpec((lq, kd), prev(vcol)), pl.BlockSpec((lq, kd), cur(vcol))],
        out_specs=pl.BlockSpec((lq, qd), cur(0)),
    )
    return pl.pallas_call(
        functools.partial(_swa_prompt_kernel, hk=hk, grp=hq // hk, hd=hd, win_chunks=win_chunks),
        grid_spec=grid_spec,
        out_shape=jax.ShapeDtypeStruct((t, qd), BF16),
        compiler_params=_cparams(("parallel", "arbitrary")),
        name="swa_prompt",
    )(sink, *([qkv] * 5))


def _swa_sample(qkv, sink, ck, cv, o_buf, *, row0, nb, seq, hq, hk, hd):
    qd, kd = hq * hd, hk * hd
    win = ck.shape[1]
    rb0 = row0 // seq
    grid_spec = pltpu.PrefetchScalarGridSpec(
        num_scalar_prefetch=1,
        grid=(nb,),
        in_specs=[
            pl.BlockSpec((seq, qd), lambda b, s: (rb0 + b, 0)),
            pl.BlockSpec((seq, kd), lambda b, s: (rb0 + b, qd // kd)),
            pl.BlockSpec((seq, kd), lambda b, s: (rb0 + b, qd // kd + 1)),
            pl.BlockSpec((1, win, kd), lambda b, s: (b, 0, 0)),
            pl.BlockSpec((1, win, kd), lambda b, s: (b, 0, 0)),
            _ANY,
        ],
        out_specs=pl.BlockSpec((seq, qd), lambda b, s: (rb0 + b, 0)),
    )
    return pl.pallas_call(
        functools.partial(_swa_sample_kernel, hk=hk, grp=hq // hk, hd=hd),
        grid_spec=grid_spec,
        out_shape=jax.ShapeDtypeStruct(o_buf.shape, o_buf.dtype),
        input_output_aliases={6: 0},
        compiler_params=_cparams(("parallel",)),
        name="swa_sample",
    )(sink, qkv, qkv, qkv, ck.reshape(nb, win, kd), cv.reshape(nb, win, kd), o_buf)


def _pool_kernel(*refs, has_hist, n_alias, pos0):
    x_ref, g_ref = refs[:2]
    k0 = 2
    if has_hist:
        hist_ref = refs[k0]
        k0 += 1
    w_ref, sc_ref = refs[k0], refs[k0 + 1]
    o_ref, st_ref, hb = refs[k0 + 2 + n_alias:]
    l = pl.program_id(1)
    tm, d = x_ref.shape
    gd = w_ref.shape[1]

    @pl.when(l == 0)
    def _():
        if has_hist:
            hb[0:HIST_ROWS - POOL_STATE, :] = jnp.zeros((HIST_ROWS - POOL_STATE, d), F32)
            hb[HIST_ROWS - POOL_STATE:HIST_ROWS, :] = hist_ref[0]
        else:
            hb[0:HIST_ROWS, :] = jnp.zeros((HIST_ROWS, d), F32)

    x = x_ref[...]
    hb[HIST_ROWS:HIST_ROWS + tm, :] = _rms(x, g_ref[...])
    pos = pos0 + l * tm + lax.broadcasted_iota(jnp.int32, (tm, 1), 0)
    for gi, w in enumerate(POOL_WINDOWS):
        cs = slice(gi * gd, (gi + 1) * gd)
        hcur = hb[HIST_ROWS:HIST_ROWS + tm, cs]
        acc = hcur
        for j in range(1, w):
            acc = acc + hb[HIST_ROWS - j:HIST_ROWS - j + tm, cs]
        cnt = jnp.minimum(w, pos + 1).astype(F32)
        pooled = acc / cnt - hcur
        mixed = _dot(pooled.astype(BF16), w_ref[gi])
        o_ref[:, cs] = x[:, cs] + mixed * sc_ref[:, cs]
    tail = hb[tm + HIST_ROWS - POOL_STATE:tm + HIST_ROWS, :]
    hb[HIST_ROWS - POOL_STATE:HIST_ROWS, :] = tail

    @pl.when(l == pl.num_programs(1) - 1)
    def _():
        st_ref[0] = tail


def _pool(x, g, hist, w, scale, *, row0, nb, seq, pos0, o_buf=None):
    t, d = x.shape
    ng, gd = w.shape[0], w.shape[1]
    tm = _pick(seq, (512, 256, 128, 64, 32))
    nl = seq // tm
    rb0 = row0 // tm
    assert row0 % tm == 0 and ng == len(POOL_WINDOWS) and tm >= POOL_STATE
    rows = lambda b, l: (rb0 + b * nl + l, 0)
    in_specs = [pl.BlockSpec((tm, d), rows), pl.BlockSpec((1, d), lambda b, l: (0, 0))]
    args = [x, g.reshape(1, d)]
    if hist is not None:
        in_specs.append(pl.BlockSpec((1, POOL_STATE, d), lambda b, l: (b, 0, 0)))
        args.append(hist)
    in_specs += [pl.BlockSpec((ng, gd, gd), lambda b, l: (0, 0, 0)), pl.BlockSpec((1, d), lambda b, l: (0, 0))]
    args += [w, scale.reshape(1, d)]
    aliases = {}
    if o_buf is not None:
        aliases[len(args)] = 0
        in_specs.append(_ANY)
        args.append(o_buf)
    return pl.pallas_call(
        functools.partial(_pool_kernel, has_hist=hist is not None, n_alias=len(aliases), pos0=pos0),
        grid=(nb, nl),
        in_specs=in_specs,
        out_specs=[pl.BlockSpec((tm, d), rows), pl.BlockSpec((1, POOL_STATE, d), lambda b, l: (b, 0, 0))],
        out_shape=[jax.ShapeDtypeStruct((t, d), F32), jax.ShapeDtypeStruct((nb, POOL_STATE, d), F32)],
        scratch_shapes=[pltpu.VMEM((HIST_ROWS + tm, d), F32)],
        input_output_aliases=aliases,
        compiler_params=_cparams(("parallel", "arbitrary")),
        name="pool_mixer",
    )(*args)


def _router_kernel(x_ref, g_ref, whi_ref, wlo_ref, b_ref, h_ref, info_ref, cnt_ref, carry, *, ng, epg):
    i = pl.program_id(0)
    tm = x_ref.shape[0]

    @pl.when(i == 0)
    def _():
        carry[...] = jnp.zeros_like(carry)

    h = _rms(x_ref[...], g_ref[...])
    h_ref[...] = h
    h_hi = h.astype(BF16)
    h_lo = (h - h_hi.astype(F32)).astype(BF16)
    logits = (_dot(h_hi, whi_ref[...]) + (_dot(h_hi, wlo_ref[...]) + _dot(h_lo, whi_ref[...]))) + b_ref[...]
    lane = lax.broadcasted_iota(jnp.int32, logits.shape, 1)
    ninf = -jnp.inf
    big = jnp.int32(LANES)
    is_g = lane < ng
    gl = jnp.where(is_g, logits, ninf)
    gmax = jnp.max(gl, axis=-1, keepdims=True)
    gidx = jnp.min(jnp.where(gl == gmax, lane, big), axis=-1, keepdims=True)
    g_w = 1.0 / jnp.sum(jnp.where(is_g, jnp.exp(gl - gmax), 0.0), axis=-1, keepdims=True)
    lo = ng + gidx * epg
    el = jnp.where((lane >= lo) & (lane < lo + epg), logits, ninf)
    v1 = jnp.max(el, axis=-1, keepdims=True)
    i1 = jnp.min(jnp.where(el == v1, lane, big), axis=-1, keepdims=True)
    el2 = jnp.where(lane == i1, ninf, el)
    v2 = jnp.max(el2, axis=-1, keepdims=True)
    i2 = jnp.min(jnp.where(el2 == v2, lane, big), axis=-1, keepdims=True)
    e = jnp.exp(v2 - v1)
    w1 = (1.0 / (1.0 + e)) * g_w
    w2 = (e / (1.0 + e)) * g_w
    e1, e2 = i1 - ng, i2 - ng
    hot = (lane == e1) | (lane == e2)
    row = lax.broadcasted_iota(jnp.int32, (tm, tm), 0)
    col = lax.broadcasted_iota(jnp.int32, (tm, tm), 1)
    before = _dot((col < row).astype(BF16), hot.astype(BF16)) + carry[...]
    r1 = jnp.sum(jnp.where(lane == e1, before, 0.0), axis=-1, keepdims=True)
    r2 = jnp.sum(jnp.where(lane == e2, before, 0.0), axis=-1, keepdims=True)
    fields = (e1.astype(F32), e2.astype(F32), w1, w2, r1, r2)
    info = jnp.zeros(logits.shape, F32)
    for n, f in enumerate(fields):
        info = jnp.where(lane == n, f, info)
    info_ref[...] = info
    total = carry[...] + jnp.sum(hot.astype(F32), axis=0, keepdims=True)
    carry[...] = total
    cnt_ref[...] = total


def _router(x, g, w_cat, b_cat, *, ng, epg):
    t, d = x.shape
    tm = _pick(t, (512, 256, 128))
    w_hi = w_cat.astype(BF16)
    w_lo = (w_cat - w_hi.astype(F32)).astype(BF16)
    return pl.pallas_call(
        functools.partial(_router_kernel, ng=ng, epg=epg),
        grid=(t // tm,),
        in_specs=[
            pl.BlockSpec((tm, d), lambda i: (i, 0)),
            pl.BlockSpec((1, d), lambda i: (0, 0)),
            pl.BlockSpec((d, LANES), lambda i: (0, 0)),
            pl.BlockSpec((d, LANES), lambda i: (0, 0)),
            pl.BlockSpec((1, LANES), lambda i: (0, 0)),
        ],
        out_specs=[
            pl.BlockSpec((tm, d), lambda i: (i, 0)),
            pl.BlockSpec((tm, LANES), lambda i: (i, 0)),
            pl.BlockSpec((1, LANES), lambda i: (0, 0)),
        ],
        out_shape=[jax.ShapeDtypeStruct((t, d), F32), jax.ShapeDtypeStruct((t, LANES), F32),
                   jax.ShapeDtypeStruct((1, LANES), F32)],
        scratch_shapes=[pltpu.VMEM((1, LANES), F32)],
        compiler_params=_cparams(("arbitrary",)),
        name="moe_router",
    )(x, g.reshape(1, d), w_hi, w_lo, b_cat)


def _row_copy(src_hbm, row, dst, r, sem):
    return pltpu.make_async_copy(src_hbm.at[pl.ds(row, 1), :], dst.at[pl.ds(r, 1), :], sem)


def _gather_rows(src_hbm, idx_ref, base, dst, sem, r0, n, start):
    if not start:
        pltpu.make_async_copy(src_hbm.at[pl.ds(0, n), :], dst.at[pl.ds(r0, n), :], sem).wait()
        return
    for r in range(r0, r0 + n):
        _row_copy(src_hbm, idx_ref[base + r], dst, r, sem).start(priority=GATHER_DMA_PRIORITY)


def _experts_kernel(te_ref, src_ref, nu_ref, h_hbm, wg_ref, wu_ref, wd_ref, y_ref, buf, wgb, wub, wdb, sem):
    i = pl.program_id(0)
    n = pl.num_programs(0)
    tm, d = y_ref.shape
    ff = wgb.shape[1]
    n_used = nu_ref[0]
    slot = i % 2
    nslot = (i + 1) % 2
    nxt = jnp.where(i + 1 < n, i + 1, 0)

    @pl.when(i == 0)
    def _():
        _gather_rows(h_hbm, src_ref, 0, buf.at[0], sem.at[0], 0, tm, True)

    _gather_rows(h_hbm, src_ref, 0, buf.at[slot], sem.at[slot], 0, tm, False)

    @pl.when(i < n_used)
    def _():
        @pl.when((i == 0) | (te_ref[i] != te_ref[jnp.maximum(i - 1, 0)]))
        def _():
            wgb[...] = wg_ref[...].astype(BF16)
            wub[...] = wu_ref[...].astype(BF16)
            wdb[...] = wd_ref[...].astype(BF16)

        up_cols = min(ff, MM_SUB)
        n_up, n_down = ff // up_cols, 2
        down_cols = d // n_down
        edges = [(g * tm) // (n_up + n_down) for g in range(n_up + n_down + 1)]
        issue = lambda g: _gather_rows(h_hbm, src_ref, nxt * tm, buf.at[nslot], sem.at[nslot],
                                       edges[g], edges[g + 1] - edges[g], True)
        xb = buf[slot].astype(BF16)
        parts = []
        for c in range(n_up):
            issue(c)
            cs = slice(c * up_cols, (c + 1) * up_cols)
            gt = _dot(xb, wgb[:, cs])
            parts.append(((gt * _sigmoid(gt)) * _dot(xb, wub[:, cs])).astype(BF16))
        a = jnp.concatenate(parts, axis=1)
        for c in range(n_down):
            issue(n_up + c)
            cs = slice(c * down_cols, (c + 1) * down_cols)
            y_ref[:, cs] = _dot(a, wdb[:, cs])

    @pl.when(i >= n_used)
    def _():
        _gather_rows(h_hbm, src_ref, nxt * tm, buf.at[nslot], sem.at[nslot], 0, tm, True)
        y_ref[...] = jnp.zeros_like(y_ref)

    @pl.when(i == n - 1)
    def _():
        _gather_rows(h_hbm, src_ref, 0, buf.at[nslot], sem.at[nslot], 0, tm, False)


def _experts(h, tile_expert, src, n_used, wg, wu, wd, layer):
    d = h.shape[1]
    ff = wg.shape[3]
    tm = EXPERT_TILE
    p = src.shape[0]
    grid_spec = pltpu.PrefetchScalarGridSpec(
        num_scalar_prefetch=3,
        grid=(p // tm,),
        in_specs=[
            _ANY,
            pl.BlockSpec((None, None, d, ff), lambda i, te, s, nu: (layer, te[i], 0, 0)),
            pl.BlockSpec((None, None, d, ff), lambda i, te, s, nu: (layer, te[i], 0, 0)),
            pl.BlockSpec((None, None, ff, d), lambda i, te, s, nu: (layer, te[i], 0, 0)),
        ],
        out_specs=pl.BlockSpec((tm, d), lambda i, te, s, nu: (i, 0)),
        scratch_shapes=[pltpu.VMEM((2, tm, d), F32), pltpu.VMEM((d, ff), BF16), pltpu.VMEM((d, ff), BF16),
                        pltpu.VMEM((ff, d), BF16), pltpu.SemaphoreType.DMA((2,))],
    )
    return pl.pallas_call(
        _experts_kernel,
        grid_spec=grid_spec,
        out_shape=jax.ShapeDtypeStruct((p, d), F32),
        compiler_params=_cparams(("arbitrary",)),
        name="moe_experts",
    )(tile_expert, src, n_used, h, wg, wu, wd)


def _ple_kernel(p0_ref, p1_ref, x_ref, info_ref, p_ref, y_hbm, g_ref, wup_ref, wgt_ref, gn_ref,
                xo_ref, hn_ref, ybuf, sem):
    i = pl.program_id(0)
    n = pl.num_programs(0)
    tm = x_ref.shape[0]

    d = x_ref.shape[1]
    slot = i % 2
    nslot = (i + 1) % 2
    nxt = jnp.where(i + 1 < n, i + 1, 0)

    @pl.when(i == 0)
    def _():
        for k, pref in enumerate((p0_ref, p1_ref)):
            _gather_rows(y_hbm, pref, 0, ybuf.at[0, k], sem.at[0], 0, tm, True)

    for k in range(2):
        _gather_rows(y_hbm, p0_ref, 0, ybuf.at[slot, k], sem.at[slot], 0, tm, False)
    info = info_ref[...]
    x2 = x_ref[...] + info[:, 2:3] * ybuf[slot, 0] + info[:, 3:4] * ybuf[slot, 1]
    h3 = _rms(x2, g_ref[...]).astype(BF16)
    pe = _dot(p_ref[...].astype(BF16), wup_ref[...])
    rows_per = tm // PLE_GROUPS
    cols_per = d // PLE_GROUPS
    parts = []
    for c in range(PLE_GROUPS):
        for k, pref in enumerate((p0_ref, p1_ref)):
            _gather_rows(y_hbm, pref, nxt * tm, ybuf.at[nslot, k], sem.at[nslot], c * rows_per, rows_per, True)
        cs = slice(c * cols_per, (c + 1) * cols_per)
        parts.append(x2[:, cs] + pe[:, cs] * _sigmoid(_dot(h3, wgt_ref[:, cs])))
    x3 = jnp.concatenate(parts, axis=1)
    xo_ref[...] = x3
    hn_ref[...] = _rms(x3, gn_ref[...]).astype(hn_ref.dtype)

    @pl.when(i == n - 1)
    def _():
        for k in range(2):
            _gather_rows(y_hbm, p0_ref, 0, ybuf.at[nslot, k], sem.at[nslot], 0, tm, False)


def _combine_ple(x, info, p, y, pos0, pos1, g, wup, wgt, gn, hn_dtype):
    t, d = x.shape
    pd = p.shape[1]
    tm = _pick(t, (256, 128))
    const = lambda i, a, b: (0, 0)
    grid_spec = pltpu.PrefetchScalarGridSpec(
        num_scalar_prefetch=2,
        grid=(t // tm,),
        in_specs=[
            pl.BlockSpec((tm, d), lambda i, a, b: (i, 0)),
            pl.BlockSpec((tm, LANES), lambda i, a, b: (i, 0)),
            pl.BlockSpec((tm, pd), lambda i, a, b: (i, 0)),
            _ANY,
            pl.BlockSpec((1, d), const),
            pl.BlockSpec((pd, d), const),
            pl.BlockSpec((d, d), const),
            pl.BlockSpec((1, d), const),
        ],
        out_specs=[pl.BlockSpec((tm, d), lambda i, a, b: (i, 0)), pl.BlockSpec((tm, d), lambda i, a, b: (i, 0))],
        scratch_shapes=[pltpu.VMEM((2, 2, tm, d), F32), pltpu.SemaphoreType.DMA((2,))],
    )
    return pl.pallas_call(
        _ple_kernel,
        grid_spec=grid_spec,
        out_shape=[jax.ShapeDtypeStruct((t, d), F32), jax.ShapeDtypeStruct((t, d), hn_dtype)],
        compiler_params=_cparams(("arbitrary",)),
        name="moe_combine_ple",
    )(pos0, pos1, x, info, p, y, g.reshape(1, d), wup, wgt, gn.reshape(1, d))


def _moe_ple(x1, p, layer, g_ffn, w_group, b_group, w_router, b_router, w_gate, w_up, w_down, g_ple, ple_up, ple_gate,
             g_next, hn_dtype):
    t, d = x1.shape
    ng = w_group.shape[1]
    ne = w_router.shape[1]
    epg = ne // ng
    assert ng + ne <= LANES
    w_cat = jnp.zeros((d, LANES), F32).at[:, :ng].set(w_group).at[:, ng:ng + ne].set(w_router)
    b_cat = jnp.zeros((1, LANES), F32).at[0, :ng].set(b_group).at[0, ng:ng + ne].set(b_router)
    h2, info, cnt = _router(x1, g_ffn, w_cat, b_cat, ng=ng, epg=epg)

    tm = EXPERT_TILE
    n_tiles = (2 * t) // tm + ne
    counts = cnt[0, :ne].astype(jnp.int32)
    padded = ((counts + tm - 1) // tm) * tm
    ends = jnp.cumsum(padded)
    offs = ends - padded
    eid = info[:, 0:2].astype(jnp.int32)
    rank = info[:, 4:6].astype(jnp.int32)
    pos = offs[eid] + rank
    n_used = (ends[-1] // tm).astype(jnp.int32).reshape(1)
    tile_start = jnp.arange(n_tiles, dtype=jnp.int32) * tm
    tile_expert = jnp.minimum(jnp.sum((tile_start[:, None] >= ends[None, :]).astype(jnp.int32), axis=1), ne - 1)
    tok = jnp.broadcast_to(jnp.arange(t, dtype=jnp.int32)[:, None], (t, 2))
    src = jnp.zeros((n_tiles * tm,), jnp.int32).at[pos.reshape(-1)].set(tok.reshape(-1))

    y = _experts(h2, tile_expert, src, n_used, w_gate, w_up, w_down, layer)
    return _combine_ple(x1, info, p, y, pos[:, 0], pos[:, 1], g_ple, ple_up, ple_gate, g_next, hn_dtype)


def kernel(x_prompt, x_sample, state_ret, cache_swa_k, cache_swa_v, state_pool, p_prompt, p_sample, norm_mix, norm_ffn, norm_ple, norm_final, ret_w_in, ret_gn, ret_w_out, swa_w_in, swa_sink, swa_w_out, pool_w, pool_scale, moe_w_group, moe_b_group, moe_w_router, moe_b_router, moe_w_gate, moe_w_up, moe_w_down, ple_w_up, ple_w_gate):
    bp, lp, d = x_prompt.shape
    bs, ls, _ = x_sample.shape
    tp, ts = bp * lp, bs * ls
    depth = norm_mix.shape[0]
    pd = p_prompt.shape[-1]
    n_ret, _, nh, dk, dv = state_ret.shape
    _, _, win, hk, hd = cache_swa_k.shape
    hq = swa_sink.shape[1]
    assert ls <= CHUNK and ls <= win and lp % CHUNK == 0 and lp >= win and win % CHUNK == 0

    x = jnp.concatenate([x_prompt.reshape(tp, d), x_sample.reshape(ts, d)], axis=0)
    pos = jnp.concatenate([jnp.tile(jnp.arange(lp, dtype=jnp.int32), bp),
                           jnp.tile(PAST_LEN + jnp.arange(ls, dtype=jnp.int32), bs)])
    lg = jnp.log1p(-jnp.exp2(-5.0 - jnp.arange(nh, dtype=F32)))
    ret_cl = _pick(lp, (256, 128, 64))
    tn_d = _pick(d, (512, 256, 128))
    ple_up, ple_gate = ple_w_up.astype(BF16), ple_w_gate.astype(BF16)
    pool_wb = pool_w.astype(BF16)

    hn = _norm(x, norm_mix[0], BF16)
    ret_p = ret_s = None
    k_p, k_s, v_p, v_s, pool_p, pool_s = [], [], [], [], [], []
    for i in range(depth):
        kind, j = i % 3, i // 3
        if kind == 0:
            qkw = 2 * nh * dk + nh * dv
            tn_in = _pick(nh * dk, (1024, 512, 256))
            qkv = _matmul(hn, ret_w_in, j, tn=tn_in, ncols=qkw, rope=(dk, 2 * nh * dk, nh * dk, dk ** -0.5),
                          tables=_rope_tables(pos, dk), out_dtype=BF16, name="ret_in_proj")
            gate = _matmul(hn, ret_w_in, j, tn=tn_in, col0=qkw, name="ret_gate_proj")
            o, ret_p = _retention(qkv, gate, None, ret_gn[j], lg, layer=j, n_layers=n_ret, row0=0, nb=bp, seq=lp,
                                  cl=ret_cl, hb=2, st_buf=ret_p)
            o, ret_s = _retention(qkv, gate, state_ret, ret_gn[j], lg, layer=j, n_layers=n_ret, row0=tp, nb=bs, seq=ls,
                                  cl=ls, hb=nh, o_buf=o, st_buf=ret_s)
            x1 = _matmul(o, ret_w_out, j, tn=tn_d, res=x, name="ret_out_proj")
        elif kind == 1:
            qd, kd = hq * hd, hk * hd
            tn = _pick(kd, (512, 256, 128))
            qkv = _matmul(hn, swa_w_in, j, tn=tn, rope=(hd, qd + kd, 0, 1.0),
                          tables=_rope_tables(pos, hd), name="swa_in_proj")
            o = _swa_prompt(qkv, swa_sink[j], nb=bp, seq=lp, win=win, hq=hq, hk=hk, hd=hd)
            o = _swa_sample(qkv, swa_sink[j], cache_swa_k[j], cache_swa_v[j], o, row0=tp, nb=bs, seq=ls,
                            hq=hq, hk=hk, hd=hd)
            kv_p = jnp.stack([lax.slice(qkv, (b * lp + lp - win, qd), ((b + 1) * lp, qd + 2 * kd)) for b in range(bp)])
            kv_s = lax.slice(qkv, (tp, qd), (tp + ts, qd + 2 * kd)).reshape(bs, ls, 2 * kd)
            k_p.append(kv_p[..., :kd].reshape(bp, win, hk, hd))
            v_p.append(kv_p[..., kd:].reshape(bp, win, hk, hd))
            k_s.append(jnp.concatenate([cache_swa_k[j][:, ls:], kv_s[..., :kd].reshape(bs, ls, hk, hd)], axis=1))
            v_s.append(jnp.concatenate([cache_swa_v[j][:, ls:], kv_s[..., kd:].reshape(bs, ls, hk, hd)], axis=1))
            x1 = _matmul(o, swa_w_out, j, tn=tn_d, res=x, name="swa_out_proj")
        else:
            x1, st_p = _pool(x, norm_mix[i], None, pool_wb[j], pool_scale[j], row0=0, nb=bp, seq=lp, pos0=0)
            x1, st_s = _pool(x, norm_mix[i], state_pool[j], pool_wb[j], pool_scale[j], row0=tp, nb=bs, seq=ls,
                             pos0=PAST_LEN, o_buf=x1)
            pool_p.append(st_p)
            pool_s.append(st_s)
        p = jnp.concatenate([p_prompt[i].reshape(tp, pd), p_sample[i].reshape(ts, pd)], axis=0)
        last = i == depth - 1
        g_next = norm_final if last else norm_mix[i + 1]
        x, hn = _moe_ple(x1, p, i, norm_ffn[i], moe_w_group[i], moe_b_group[i], moe_w_router[i], moe_b_router[i],
                         moe_w_gate, moe_w_up, moe_w_down, norm_ple[i], ple_up[i], ple_gate[i],
                         g_next, F32 if last else BF16)
    y = hn
    return (y[:tp].reshape(bp, lp, d), y[tp:].reshape(bs, ls, d), ret_p, ret_s,
            jnp.stack(k_p), jnp.stack(k_s), jnp.stack(v_p), jnp.stack(v_s), jnp.stack(pool_p), jnp.stack(pool_s))
```

```python
import functools

import jax
import jax.numpy as jnp
from jax import lax
from jax.experimental import pallas as pl
from jax.experimental.pallas import tpu as pltpu

F32 = jnp.float32
BF16 = jnp.bfloat16
EPS = 1e-6
ROPE_THETA = 10000.0
PAST_LEN = 4096
CHUNK = 64
POOL_WINDOWS = (2, 4, 8, 16)
POOL_STATE = POOL_WINDOWS[-1] - 1
HIST_ROWS = 16
VMEM_LIMIT = 56 * 1024 * 1024
EXPERT_TILE = 256
LANES = 128
MM_SUB = 256
COMBINE_DMA_PRIORITY = 1
PLE_GROUPS = 8


def _pick(n, cands):
    for c in cands:
        if n % c == 0:
            return c
    raise ValueError(f"no tile in {cands} divides {n}")


def _cparams(sem):
    return pltpu.CompilerParams(dimension_semantics=sem, vmem_limit_bytes=VMEM_LIMIT)


def _rms(x, g):
    return x * lax.rsqrt(jnp.mean(x * x, axis=-1, keepdims=True) + EPS) * g


def _dot(a, b):
    return jnp.dot(a, b, preferred_element_type=F32)


def _dot_nt(a, b):
    return lax.dot_general(a, b, (((1,), (1,)), ((), ())), preferred_element_type=F32)


def _dot_tn(a, b):
    return lax.dot_general(a, b, (((0,), (0,)), ((), ())), preferred_element_type=F32)


def _sigmoid(x):
    return 1.0 / (1.0 + jnp.exp(-x))


_ANY = pl.BlockSpec(memory_space=pl.ANY)


def _norm_kernel(x_ref, g_ref, o_ref):
    o_ref[...] = _rms(x_ref[...], g_ref[...]).astype(o_ref.dtype)


def _norm(x, g, out_dtype):
    t, d = x.shape
    tm = _pick(t, (512, 256, 128))
    return pl.pallas_call(
        _norm_kernel,
        grid=(t // tm,),
        in_specs=[pl.BlockSpec((tm, d), lambda i: (i, 0)), pl.BlockSpec((1, d), lambda i: (0, 0))],
        out_specs=pl.BlockSpec((tm, d), lambda i: (i, 0)),
        out_shape=jax.ShapeDtypeStruct((t, d), out_dtype),
        compiler_params=_cparams(("parallel",)),
        name="rmsnorm",
    )(x, g.reshape(1, d))


def _mm_kernel(*refs, has_res, rope):
    a_ref, w_ref = refs[0], refs[1]
    k = 2
    if has_res:
        r_ref = refs[k]
        k += 1
    if rope is not None:
        cos_ref, sin_ref = refs[k], refs[k + 1]
        k += 2
    o_ref = refs[k]
    tn = o_ref.shape[1]
    sub = min(tn, MM_SUB)
    a = a_ref[...]
    j = pl.program_id(1)
    for c0 in range(0, tn, sub):
        cs = slice(c0, c0 + sub)
        acc = _dot(a, w_ref[:, cs].astype(BF16))
        if has_res:
            acc = acc + r_ref[:, cs]
        if rope is not None:
            hd, rope_cols, q_cols, q_scale = rope
            half = hd // 2
            col = j * tn + c0
            rep = sub // cos_ref.shape[1]
            c, s = cos_ref[...], sin_ref[...]
            if rep > 1:
                c, s = jnp.tile(c, (1, rep)), jnp.tile(s, (1, rep))
            lane = lax.broadcasted_iota(jnp.int32, acc.shape, 1)
            partner = jnp.where((lane % hd) < half, pltpu.roll(acc, sub - half, 1), pltpu.roll(acc, half, 1))
            y = acc * c + partner * s
            if q_scale != 1.0:
                y = y * jnp.where(col < q_cols, q_scale, 1.0)
            acc = jnp.where(col < rope_cols, y, acc)
        o_ref[:, cs] = acc.astype(o_ref.dtype)


def _matmul(a, w, layer, *, tn, col0=0, ncols=None, res=None, rope=None, tables=None, out_dtype=F32, name="matmul"):
    m, kd = a.shape
    n = w.shape[2] - col0 if ncols is None else ncols
    assert col0 % tn == 0 and n % tn == 0
    cb0 = col0 // tn
    tm = _pick(m, (1024, 512, 256, 128))
    in_specs = [pl.BlockSpec((tm, kd), lambda i, j: (i, 0)),
                pl.BlockSpec((None, kd, tn), lambda i, j: (layer, 0, cb0 + j))]
    args = [a, w]
    if res is not None:
        in_specs.append(pl.BlockSpec((tm, tn), lambda i, j: (i, j)))
        args.append(res)
    if rope is not None:
        tw = tables[0].shape[1]
        in_specs += [pl.BlockSpec((tm, tw), lambda i, j: (i, 0))] * 2
        args += list(tables)
    return pl.pallas_call(
        functools.partial(_mm_kernel, has_res=res is not None, rope=rope),
        grid=(m // tm, n // tn),
        in_specs=in_specs,
        out_specs=pl.BlockSpec((tm, tn), lambda i, j: (i, j)),
        out_shape=jax.ShapeDtypeStruct((m, n), out_dtype),
        compiler_params=_cparams(("parallel", "arbitrary")),
        name=name,
    )(*args)


def _rope_tables(pos, hd):
    half = hd // 2
    inv = 1.0 / jnp.power(ROPE_THETA, jnp.arange(half, dtype=F32) / half)
    ang = pos.astype(F32)[:, None] * inv[None, :]
    c, s = jnp.cos(ang), jnp.sin(ang)
    cosf = jnp.concatenate([c, c], axis=-1)
    sinf = jnp.concatenate([-s, s], axis=-1)
    rep = max(LANES // hd, 1)
    return jnp.tile(cosf, (1, rep)), jnp.tile(sinf, (1, rep))


def _ret_kernel(*refs, has_s0, n_alias):
    lg_ref, q_ref, k_ref, v_ref, g_ref = refs[:5]
    k0 = 5
    if has_s0:
        s0_ref = refs[k0]
        k0 += 1
    gn_ref = refs[k0]
    o_ref, sn_ref, s_scr, dec_scr = refs[k0 + 1 + n_alias:]
    hblk, c = pl.program_id(1), pl.program_id(2)
    cl = q_ref.shape[0]
    hb, dk, dv = s_scr.shape
    idx = lax.broadcasted_iota(jnp.int32, (cl, 1), 0).astype(F32)
    for hh in range(hb):
        lg = lg_ref[hblk * hb + hh]
        ks, vs = slice(hh * dk, (hh + 1) * dk), slice(hh * dv, (hh + 1) * dv)

        @pl.when(c == 0)
        def _():
            s_scr[hh] = s0_ref[hh] if has_s0 else jnp.zeros((dk, dv), F32)
            row = lax.broadcasted_iota(jnp.int32, (cl, cl), 0)
            col = lax.broadcasted_iota(jnp.int32, (cl, cl), 1)
            diff = (row - col).astype(F32)
            dec_scr[hh] = jnp.where(diff >= 0, jnp.exp(jnp.maximum(diff, 0.0) * lg), 0.0)

    for hh in range(hb):
        lg = lg_ref[hblk * hb + hh]
        ks, vs = slice(hh * dk, (hh + 1) * dk), slice(hh * dv, (hh + 1) * dv)
        qb, kb, vb = q_ref[:, ks], k_ref[:, ks], v_ref[:, vs]
        scores = _dot_nt(qb, kb) * dec_scr[hh]
        inner = _dot(scores.astype(BF16), vb)
        s_prev = s_scr[hh]
        cross = _dot(qb, s_prev.astype(BF16)) * jnp.exp((idx + 1.0) * lg)
        k_dec = kb.astype(F32) * jnp.exp((cl - 1.0 - idx) * lg)
        chunk_decay = jnp.exp(jnp.full((1, 1), float(cl), F32) * lg)
        s_new = s_prev * chunk_decay + _dot_tn(k_dec.astype(BF16), vb)
        s_scr[hh] = s_new
        o = inner + cross
        mu = jnp.mean(o, axis=-1, keepdims=True)
        var = jnp.mean(jnp.square(o - mu), axis=-1, keepdims=True)
        on = (o - mu) * lax.rsqrt(var + EPS) * gn_ref[:, vs]
        g = g_ref[:, vs]
        o_ref[:, vs] = ((g * _sigmoid(g)) * on).astype(o_ref.dtype)

    @pl.when(c == pl.num_programs(2) - 1)
    def _():
        sn_ref[...] = s_scr[...]


def _retention(qkv, gate, s0_all, gn, lg, *, layer, n_layers, row0, nb, seq, cl, hb, o_buf=None, st_buf=None):
    t = qkv.shape[0]
    nh = lg.shape[0]
    dv = gate.shape[1] // nh
    dk = (qkv.shape[1] - nh * dv) // (2 * nh)
    assert dv == 2 * dk and row0 % cl == 0 and seq % cl == 0 and nh % hb == 0
    nc = seq // cl
    rb0 = row0 // cl
    nhb = nh // hb
    rowmap = lambda b, c: rb0 + b * nc + c
    st_spec = pl.BlockSpec((None, None, hb, dk, dv), lambda b, h, c, lg: (layer, b, h, 0, 0))
    in_specs = [
        pl.BlockSpec((cl, hb * dk), lambda b, h, c, lg: (rowmap(b, c), h)),
        pl.BlockSpec((cl, hb * dk), lambda b, h, c, lg: (rowmap(b, c), nhb + h)),
        pl.BlockSpec((cl, hb * dv), lambda b, h, c, lg: (rowmap(b, c), nhb + h)),
        pl.BlockSpec((cl, hb * dv), lambda b, h, c, lg: (rowmap(b, c), h)),
    ]
    args = [lg, qkv, qkv, qkv, gate]
    if s0_all is not None:
        in_specs.append(st_spec)
        args.append(s0_all)
    in_specs.append(pl.BlockSpec((1, hb * dv), lambda b, h, c, lg: (0, h)))
    args.append(gn.reshape(1, nh * dv))
    aliases = {}
    for out_idx, buf in enumerate((o_buf, st_buf)):
        if buf is not None:
            aliases[len(args)] = out_idx
            in_specs.append(_ANY)
            args.append(buf)
    grid_spec = pltpu.PrefetchScalarGridSpec(
        num_scalar_prefetch=1,
        grid=(nb, nhb, nc),
        in_specs=in_specs,
        out_specs=[pl.BlockSpec((cl, hb * dv), lambda b, h, c, lg: (rowmap(b, c), h)), st_spec],
        scratch_shapes=[pltpu.VMEM((hb, dk, dv), F32), pltpu.VMEM((hb, cl, cl), F32)],
    )
    return pl.pallas_call(
        functools.partial(_ret_kernel, has_s0=s0_all is not None, n_alias=len(aliases)),
        grid_spec=grid_spec,
        out_shape=[jax.ShapeDtypeStruct((t, nh * dv), BF16), jax.ShapeDtypeStruct((n_layers, nb, nh, dk, dv), F32)],
        input_output_aliases=aliases,
        compiler_params=_cparams(("parallel", "parallel", "arbitrary")),
        name="retention",
    )(*args)


def _attend(sink_ref, q_ref, k_all, v_all, bias, o_ref, *, hk, grp, hd):
    assert 2 * hd == LANES and grp % 2 == 0 and hk % 2 == 0
    lq = q_ref.shape[0]
    s_len = k_all.shape[0]
    nhq = hk * grp
    pair_rows = 2 * grp * lq
    lane = lax.broadcasted_iota(jnp.int32, (lq, LANES), 1)
    own = [(lane // hd) == a for a in range(2)]
    s_parts = []
    for kp in range(hk // 2):
        rows = []
        for hq in range(kp * 2 * grp, (kp + 1) * 2 * grp):
            a = (hq // grp) % 2
            blk = q_ref[:, (hq // 2) * LANES:(hq // 2 + 1) * LANES]
            if hq % 2 != a:
                blk = pltpu.roll(blk, hd, 1)
            rows.append(jnp.where(own[a], blk, 0.0))
        qs = jnp.concatenate(rows, axis=0).astype(BF16)
        s_parts.append(_dot_nt(qs, k_all[:, kp * LANES:(kp + 1) * LANES]))
    s = jnp.concatenate(s_parts, axis=0) * (hd ** -0.5)
    if bias is not None:
        s = (s.reshape(nhq, lq, s_len) + bias[None]).reshape(nhq * lq, s_len)
    sink = jnp.concatenate([jnp.full((lq, 1), sink_ref[hq], F32) for hq in range(nhq)], axis=0)
    m = jnp.maximum(jnp.max(s, axis=-1, keepdims=True), sink)
    e = jnp.exp(s - m)
    pr = (e / (jnp.sum(e, axis=-1, keepdims=True) + jnp.exp(sink - m))).astype(BF16)
    for kp in range(hk // 2):
        oh = _dot(pr[kp * pair_rows:(kp + 1) * pair_rows], v_all[:, kp * LANES:(kp + 1) * LANES])
        for hq in range(kp * 2 * grp, (kp + 1) * 2 * grp, 2):
            a = (hq // grp) % 2
            r0 = (hq - kp * 2 * grp) * lq
            lo, hi = oh[r0:r0 + lq], oh[r0 + lq:r0 + 2 * lq]
            if a == 1:
                lo = pltpu.roll(lo, hd, 1)
            else:
                hi = pltpu.roll(hi, hd, 1)
            o_ref[:, (hq // 2) * LANES:(hq // 2 + 1) * LANES] = jnp.where(lane < hd, lo, hi).astype(o_ref.dtype)


def _swa_prompt_kernel(sink_ref, q_ref, kp_ref, kc_ref, vp_ref, vc_ref, o_ref, *, hk, grp, hd, win_chunks):
    c = pl.program_id(1)
    lq = q_ref.shape[0]
    nqb = lq // CHUNK
    k_all = jnp.concatenate([kp_ref[...], kc_ref[...]], axis=0).astype(BF16)
    v_all = jnp.concatenate([vp_ref[...], vc_ref[...]], axis=0).astype(BF16)
    qc = lax.broadcasted_iota(jnp.int32, (lq, 2 * lq), 0) // CHUNK
    kc = lax.broadcasted_iota(jnp.int32, (lq, 2 * lq), 1) // CHUNK
    back = qc + nqb - kc
    valid = (back >= 0) & (back <= win_chunks) & ((c - 1) * nqb + kc >= 0)
    bias = jnp.where(valid, 0.0, -jnp.inf).astype(F32)
    _attend(sink_ref, q_ref, k_all, v_all, bias, o_ref, hk=hk, grp=grp, hd=hd)


def _swa_sample_kernel(sink_ref, q_ref, kn_ref, vn_ref, kc_ref, vc_ref, o_alias, o_ref, *, hk, grp, hd):
    del o_alias
    k_all = jnp.concatenate([kc_ref[0], kn_ref[...]], axis=0).astype(BF16)
    v_all = jnp.concatenate([vc_ref[0], vn_ref[...]], axis=0).astype(BF16)
    _attend(sink_ref, q_ref, k_all, v_all, None, o_ref, hk=hk, grp=grp, hd=hd)


def _swa_prompt(qkv, sink, *, nb, seq, win, hq, hk, hd):
    t = qkv.shape[0]
    qd, kd = hq * hd, hk * hd
    win_chunks = win // CHUNK
    lq = _pick(seq, (2 * CHUNK, CHUNK))
    assert qd % kd == 0 and lq // CHUNK >= win_chunks
    nblk = seq // lq
    kcol, vcol = qd // kd, qd // kd + 1
    cur = lambda col: (lambda b, c, s: (b * nblk + c, col))
    prev = lambda col: (lambda b, c, s: (b * nblk + jnp.maximum(c - 1, 0), col))
    grid_spec = pltpu.PrefetchScalarGridSpec(
        num_scalar_prefetch=1,
        grid=(nb, nblk),
        in_specs=[pl.BlockSpec((lq, qd), cur(0)),
                  pl.BlockSpec((lq, kd), prev(kcol)), pl.BlockSpec((lq, kd), cur(kcol)),
                  pl.BlockSpec((lq, kd), prev(vcol)), pl.BlockSpec((lq, kd), cur(vcol))],
        out_specs=pl.BlockSpec((lq, qd), cur(0)),
    )
    return pl.pallas_call(
        functools.partial(_swa_prompt_kernel, hk=hk, grp=hq // hk, hd=hd, win_chunks=win_chunks),
        grid_spec=grid_spec,
        out_shape=jax.ShapeDtypeStruct((t, qd), BF16),
        compiler_params=_cparams(("parallel", "arbitrary")),
        name="swa_prompt",
    )(sink, *([qkv] * 5))


def _swa_sample(qkv, sink, ck, cv, o_buf, *, row0, nb, seq, hq, hk, hd):
    qd, kd = hq * hd, hk * hd
    win = ck.shape[1]
    rb0 = row0 // seq
    grid_spec = pltpu.PrefetchScalarGridSpec(
        num_scalar_prefetch=1,
        grid=(nb,),
        in_specs=[
            pl.BlockSpec((seq, qd), lambda b, s: (rb0 + b, 0)),
            pl.BlockSpec((seq, kd), lambda b, s: (rb0 + b, qd // kd)),
            pl.BlockSpec((seq, kd), lambda b, s: (rb0 + b, qd // kd + 1)),
            pl.BlockSpec((1, win, kd), lambda b, s: (b, 0, 0)),
            pl.BlockSpec((1, win, kd), lambda b, s: (b, 0, 0)),
            _ANY,
        ],
        out_specs=pl.BlockSpec((seq, qd), lambda b, s: (rb0 + b, 0)),
    )
    return pl.pallas_call(
        functools.partial(_swa_sample_kernel, hk=hk, grp=hq // hk, hd=hd),
        grid_spec=grid_spec,
        out_shape=jax.ShapeDtypeStruct(o_buf.shape, o_buf.dtype),
        input_output_aliases={6: 0},
        compiler_params=_cparams(("parallel",)),
        name="swa_sample",
    )(sink, qkv, qkv, qkv, ck.reshape(nb, win, kd), cv.reshape(nb, win, kd), o_buf)


def _pool_kernel(*refs, has_hist, n_alias, pos0):
    x_ref, g_ref = refs[:2]
    k0 = 2
    if has_hist:
        hist_ref = refs[k0]
        k0 += 1
    w_ref, sc_ref = refs[k0], refs[k0 + 1]
    o_ref, st_ref, hb = refs[k0 + 2 + n_alias:]
    l = pl.program_id(1)
    tm, d = x_ref.shape
    gd = w_ref.shape[1]

    @pl.when(l == 0)
    def _():
        if has_hist:
            hb[0:HIST_ROWS - POOL_STATE, :] = jnp.zeros((HIST_ROWS - POOL_STATE, d), F32)
            hb[HIST_ROWS - POOL_STATE:HIST_ROWS, :] = hist_ref[0]
        else:
            hb[0:HIST_ROWS, :] = jnp.zeros((HIST_ROWS, d), F32)

    x = x_ref[...]
    hb[HIST_ROWS:HIST_ROWS + tm, :] = _rms(x, g_ref[...])
    pos = pos0 + l * tm + lax.broadcasted_iota(jnp.int32, (tm, 1), 0)
    for gi, w in enumerate(POOL_WINDOWS):
        cs = slice(gi * gd, (gi + 1) * gd)
        hcur = hb[HIST_ROWS:HIST_ROWS + tm, cs]
        acc = hcur
        for j in range(1, w):
            acc = acc + hb[HIST_ROWS - j:HIST_ROWS - j + tm, cs]
        cnt = jnp.minimum(w, pos + 1).astype(F32)
        pooled = acc / cnt - hcur
        mixed = _dot(pooled.astype(BF16), w_ref[gi])
        o_ref[:, cs] = x[:, cs] + mixed * sc_ref[:, cs]
    tail = hb[tm + HIST_ROWS - POOL_STATE:tm + HIST_ROWS, :]
    hb[HIST_ROWS - POOL_STATE:HIST_ROWS, :] = tail

    @pl.when(l == pl.num_programs(1) - 1)
    def _():
        st_ref[0] = tail


def _pool(x, g, hist, w, scale, *, row0, nb, seq, pos0, o_buf=None):
    t, d = x.shape
    ng, gd = w.shape[0], w.shape[1]
    tm = _pick(seq, (512, 256, 128, 64, 32))
    nl = seq // tm
    rb0 = row0 // tm
    assert row0 % tm == 0 and ng == len(POOL_WINDOWS) and tm >= POOL_STATE
    rows = lambda b, l: (rb0 + b * nl + l, 0)
    in_specs = [pl.BlockSpec((tm, d), rows), pl.BlockSpec((1, d), lambda b, l: (0, 0))]
    args = [x, g.reshape(1, d)]
    if hist is not None:
        in_specs.append(pl.BlockSpec((1, POOL_STATE, d), lambda b, l: (b, 0, 0)))
        args.append(hist)
    in_specs += [pl.BlockSpec((ng, gd, gd), lambda b, l: (0, 0, 0)), pl.BlockSpec((1, d), lambda b, l: (0, 0))]
    args += [w, scale.reshape(1, d)]
    aliases = {}
    if o_buf is not None:
        aliases[len(args)] = 0
        in_specs.append(_ANY)
        args.append(o_buf)
    return pl.pallas_call(
        functools.partial(_pool_kernel, has_hist=hist is not None, n_alias=len(aliases), pos0=pos0),
        grid=(nb, nl),
        in_specs=in_specs,
        out_specs=[pl.BlockSpec((tm, d), rows), pl.BlockSpec((1, POOL_STATE, d), lambda b, l: (b, 0, 0))],
        out_shape=[jax.ShapeDtypeStruct((t, d), F32), jax.ShapeDtypeStruct((nb, POOL_STATE, d), F32)],
        scratch_shapes=[pltpu.VMEM((HIST_ROWS + tm, d), F32)],
        input_output_aliases=aliases,
        compiler_params=_cparams(("parallel", "arbitrary")),
        name="pool_mixer",
    )(*args)


def _router_kernel(x_ref, g_ref, whi_ref, wlo_ref, b_ref, h_ref, info_ref, cnt_ref, carry, *, ng, epg):
    i = pl.program_id(0)
    tm = x_ref.shape[0]

    @pl.when(i == 0)
    def _():
        carry[...] = jnp.zeros_like(carry)

    h = _rms(x_ref[...], g_ref[...])
    h_ref[...] = h
    h_hi = h.astype(BF16)
    h_lo = (h - h_hi.astype(F32)).astype(BF16)
    logits = (_dot(h_hi, whi_ref[...]) + (_dot(h_hi, wlo_ref[...]) + _dot(h_lo, whi_ref[...]))) + b_ref[...]
    lane = lax.broadcasted_iota(jnp.int32, logits.shape, 1)
    ninf = -jnp.inf
    big = jnp.int32(LANES)
    is_g = lane < ng
    gl = jnp.where(is_g, logits, ninf)
    gmax = jnp.max(gl, axis=-1, keepdims=True)
    gidx = jnp.min(jnp.where(gl == gmax, lane, big), axis=-1, keepdims=True)
    g_w = 1.0 / jnp.sum(jnp.where(is_g, jnp.exp(gl - gmax), 0.0), axis=-1, keepdims=True)
    lo = ng + gidx * epg
    el = jnp.where((lane >= lo) & (lane < lo + epg), logits, ninf)
    v1 = jnp.max(el, axis=-1, keepdims=True)
    i1 = jnp.min(jnp.where(el == v1, lane, big), axis=-1, keepdims=True)
    el2 = jnp.where(lane == i1, ninf, el)
    v2 = jnp.max(el2, axis=-1, keepdims=True)
    i2 = jnp.min(jnp.where(el2 == v2, lane, big), axis=-1, keepdims=True)
    e = jnp.exp(v2 - v1)
    w1 = (1.0 / (1.0 + e)) * g_w
    w2 = (e / (1.0 + e)) * g_w
    e1, e2 = i1 - ng, i2 - ng
    hot = (lane == e1) | (lane == e2)
    row = lax.broadcasted_iota(jnp.int32, (tm, tm), 0)
    col = lax.broadcasted_iota(jnp.int32, (tm, tm), 1)
    before = _dot((col < row).astype(BF16), hot.astype(BF16)) + carry[...]
    r1 = jnp.sum(jnp.where(lane == e1, before, 0.0), axis=-1, keepdims=True)
    r2 = jnp.sum(jnp.where(lane == e2, before, 0.0), axis=-1, keepdims=True)
    fields = (e1.astype(F32), e2.astype(F32), w1, w2, r1, r2)
    info = jnp.zeros(logits.shape, F32)
    for n, f in enumerate(fields):
        info = jnp.where(lane == n, f, info)
    info_ref[...] = info
    total = carry[...] + jnp.sum(hot.astype(F32), axis=0, keepdims=True)
    carry[...] = total
    cnt_ref[...] = total


def _router(x, g, w_cat, b_cat, *, ng, epg):
    t, d = x.shape
    tm = _pick(t, (512, 256, 128))
    w_hi = w_cat.astype(BF16)
    w_lo = (w_cat - w_hi.astype(F32)).astype(BF16)
    return pl.pallas_call(
        functools.partial(_router_kernel, ng=ng, epg=epg),
        grid=(t // tm,),
        in_specs=[
            pl.BlockSpec((tm, d), lambda i: (i, 0)),
            pl.BlockSpec((1, d), lambda i: (0, 0)),
            pl.BlockSpec((d, LANES), lambda i: (0, 0)),
            pl.BlockSpec((d, LANES), lambda i: (0, 0)),
            pl.BlockSpec((1, LANES), lambda i: (0, 0)),
        ],
        out_specs=[
            pl.BlockSpec((tm, d), lambda i: (i, 0)),
            pl.BlockSpec((tm, LANES), lambda i: (i, 0)),
            pl.BlockSpec((1, LANES), lambda i: (0, 0)),
        ],
        out_shape=[jax.ShapeDtypeStruct((t, d), F32), jax.ShapeDtypeStruct((t, LANES), F32),
                   jax.ShapeDtypeStruct((1, LANES), F32)],
        scratch_shapes=[pltpu.VMEM((1, LANES), F32)],
        compiler_params=_cparams(("arbitrary",)),
        name="moe_router",
    )(x, g.reshape(1, d), w_hi, w_lo, b_cat)


def _row_copy(src_hbm, row, dst, r, sem):
    return pltpu.make_async_copy(src_hbm.at[pl.ds(row, 1), :], dst.at[pl.ds(r, 1), :], sem)


def _gather_rows(src_hbm, idx_ref, base, dst, sem, r0, n, start, priority=0):
    if not start:
        pltpu.make_async_copy(src_hbm.at[pl.ds(0, n), :], dst.at[pl.ds(r0, n), :], sem).wait()
        return
    for r in range(r0, r0 + n):
        _row_copy(src_hbm, idx_ref[base + r], dst, r, sem).start(priority=priority)


def _experts_kernel(te_ref, src_ref, nu_ref, h_hbm, wg_ref, wu_ref, wd_ref, y_ref, buf, wgb, wub, wdb, sem):
    i = pl.program_id(0)
    n = pl.num_programs(0)
    tm, d = y_ref.shape
    ff = wgb.shape[1]
    n_used = nu_ref[0]
    slot = i % 2
    nslot = (i + 1) % 2
    nxt = jnp.where(i + 1 < n, i + 1, 0)

    @pl.when(i == 0)
    def _():
        _gather_rows(h_hbm, src_ref, 0, buf.at[0], sem.at[0], 0, tm, True)

    _gather_rows(h_hbm, src_ref, 0, buf.at[slot], sem.at[slot], 0, tm, False)

    @pl.when(i < n_used)
    def _():
        @pl.when((i == 0) | (te_ref[i] != te_ref[jnp.maximum(i - 1, 0)]))
        def _():
            wgb[...] = wg_ref[...].astype(BF16)
            wub[...] = wu_ref[...].astype(BF16)
            wdb[...] = wd_ref[...].astype(BF16)

        up_cols = min(ff, MM_SUB)
        n_up, n_down = ff // up_cols, 2
        down_cols = d // n_down
        edges = [(g * tm) // n_up for g in range(n_up + 1)]
        xb = buf[slot].astype(BF16)
        parts = []
        for c in range(n_up):
            _gather_rows(h_hbm, src_ref, nxt * tm, buf.at[nslot], sem.at[nslot], edges[c], edges[c + 1] - edges[c], True)
            cs = slice(c * up_cols, (c + 1) * up_cols)
            gt = _dot(xb, wgb[:, cs])
            parts.append(((gt * _sigmoid(gt)) * _dot(xb, wub[:, cs])).astype(BF16))
        a = jnp.concatenate(parts, axis=1)
        for c in range(n_down):
            cs = slice(c * down_cols, (c + 1) * down_cols)
            y_ref[:, cs] = _dot(a, wdb[:, cs])

    @pl.when(i >= n_used)
    def _():
        _gather_rows(h_hbm, src_ref, nxt * tm, buf.at[nslot], sem.at[nslot], 0, tm, True)
        y_ref[...] = jnp.zeros_like(y_ref)

    @pl.when(i == n - 1)
    def _():
        _gather_rows(h_hbm, src_ref, 0, buf.at[nslot], sem.at[nslot], 0, tm, False)


def _experts(h, tile_expert, src, n_used, wg, wu, wd, layer):
    d = h.shape[1]
    ff = wg.shape[3]
    tm = EXPERT_TILE
    p = src.shape[0]
    grid_spec = pltpu.PrefetchScalarGridSpec(
        num_scalar_prefetch=3,
        grid=(p // tm,),
        in_specs=[
            _ANY,
            pl.BlockSpec((None, None, d, ff), lambda i, te, s, nu: (layer, te[i], 0, 0)),
            pl.BlockSpec((None, None, d, ff), lambda i, te, s, nu: (layer, te[i], 0, 0)),
            pl.BlockSpec((None, None, ff, d), lambda i, te, s, nu: (layer, te[i], 0, 0)),
        ],
        out_specs=pl.BlockSpec((tm, d), lambda i, te, s, nu: (i, 0)),
        scratch_shapes=[pltpu.VMEM((2, tm, d), F32), pltpu.VMEM((d, ff), BF16), pltpu.VMEM((d, ff), BF16),
                        pltpu.VMEM((ff, d), BF16), pltpu.SemaphoreType.DMA((2,))],
    )
    return pl.pallas_call(
        _experts_kernel,
        grid_spec=grid_spec,
        out_shape=jax.ShapeDtypeStruct((p, d), F32),
        compiler_params=_cparams(("arbitrary",)),
        name="moe_experts",
    )(tile_expert, src, n_used, h, wg, wu, wd)


def _ple_kernel(p0_ref, p1_ref, x_ref, info_ref, p_ref, y_hbm, g_ref, wup_ref, wgt_ref, gn_ref,
                xo_ref, hn_ref, ybuf, sem):
    i = pl.program_id(0)
    n = pl.num_programs(0)
    tm = x_ref.shape[0]

    d = x_ref.shape[1]
    slot = i % 2
    nslot = (i + 1) % 2
    nxt = jnp.where(i + 1 < n, i + 1, 0)

    @pl.when(i == 0)
    def _():
        for k, pref in enumerate((p0_ref, p1_ref)):
            _gather_rows(y_hbm, pref, 0, ybuf.at[0, k], sem.at[0], 0, tm, True, COMBINE_DMA_PRIORITY)

    for k in range(2):
        _gather_rows(y_hbm, p0_ref, 0, ybuf.at[slot, k], sem.at[slot], 0, tm, False)
    info = info_ref[...]
    x2 = x_ref[...] + info[:, 2:3] * ybuf[slot, 0] + info[:, 3:4] * ybuf[slot, 1]
    h3 = _rms(x2, g_ref[...]).astype(BF16)
    pe = _dot(p_ref[...].astype(BF16), wup_ref[...])
    rows_per = tm // PLE_GROUPS
    cols_per = d // PLE_GROUPS
    parts = []
    for c in range(PLE_GROUPS):
        for k, pref in enumerate((p0_ref, p1_ref)):
            _gather_rows(y_hbm, pref, nxt * tm, ybuf.at[nslot, k], sem.at[nslot], c * rows_per, rows_per, True,
                         COMBINE_DMA_PRIORITY)
        cs = slice(c * cols_per, (c + 1) * cols_per)
        parts.append(x2[:, cs] + pe[:, cs] * _sigmoid(_dot(h3, wgt_ref[:, cs])))
    x3 = jnp.concatenate(parts, axis=1)
    xo_ref[...] = x3
    hn_ref[...] = _rms(x3, gn_ref[...]).astype(hn_ref.dtype)

    @pl.when(i == n - 1)
    def _():
        for k in range(2):
            _gather_rows(y_hbm, p0_ref, 0, ybuf.at[nslot, k], sem.at[nslot], 0, tm, False)


def _combine_ple(x, info, p, y, pos0, pos1, g, wup, wgt, gn, hn_dtype):
    t, d = x.shape
    pd = p.shape[1]
    tm = _pick(t, (256, 128))
    const = lambda i, a, b: (0, 0)
    grid_spec = pltpu.PrefetchScalarGridSpec(
        num_scalar_prefetch=2,
        grid=(t // tm,),
        in_specs=[
            pl.BlockSpec((tm, d), lambda i, a, b: (i, 0)),
            pl.BlockSpec((tm, LANES), lambda i, a, b: (i, 0)),
            pl.BlockSpec((tm, pd), lambda i, a, b: (i, 0)),
            _ANY,
            pl.BlockSpec((1, d), const),
            pl.BlockSpec((pd, d), const),
            pl.BlockSpec((d, d), const),
            pl.BlockSpec((1, d), const),
        ],
        out_specs=[pl.BlockSpec((tm, d), lambda i, a, b: (i, 0)), pl.BlockSpec((tm, d), lambda i, a, b: (i, 0))],
        scratch_shapes=[pltpu.VMEM((2, 2, tm, d), F32), pltpu.SemaphoreType.DMA((2,))],
    )
    return pl.pallas_call(
        _ple_kernel,
        grid_spec=grid_spec,
        out_shape=[jax.ShapeDtypeStruct((t, d), F32), jax.ShapeDtypeStruct((t, d), hn_dtype)],
        compiler_params=_cparams(("arbitrary",)),
        name="moe_combine_ple",
    )(pos0, pos1, x, info, p, y, g.reshape(1, d), wup, wgt, gn.reshape(1, d))


def _moe_ple(x1, p, layer, g_ffn, w_group, b_group, w_router, b_router, w_gate, w_up, w_down, g_ple, ple_up, ple_gate,
             g_next, hn_dtype):
    t, d = x1.shape
    ng = w_group.shape[1]
    ne = w_router.shape[1]
    epg = ne // ng
    assert ng + ne <= LANES
    w_cat = jnp.zeros((d, LANES), F32).at[:, :ng].set(w_group).at[:, ng:ng + ne].set(w_router)
    b_cat = jnp.zeros((1, LANES), F32).at[0, :ng].set(b_group).at[0, ng:ng + ne].set(b_router)
    h2, info, cnt = _router(x1, g_ffn, w_cat, b_cat, ng=ng, epg=epg)

    tm = EXPERT_TILE
    n_tiles = (2 * t) // tm + ne
    counts = cnt[0, :ne].astype(jnp.int32)
    padded = ((counts + tm - 1) // tm) * tm
    ends = jnp.cumsum(padded)
    offs = ends - padded
    eid = info[:, 0:2].astype(jnp.int32)
    rank = info[:, 4:6].astype(jnp.int32)
    pos = offs[eid] + rank
    n_used = (ends[-1] // tm).astype(jnp.int32).reshape(1)
    tile_start = jnp.arange(n_tiles, dtype=jnp.int32) * tm
    tile_expert = jnp.minimum(jnp.sum((tile_start[:, None] >= ends[None, :]).astype(jnp.int32), axis=1), ne - 1)
    tok = jnp.broadcast_to(jnp.arange(t, dtype=jnp.int32)[:, None], (t, 2))
    src = jnp.zeros((n_tiles * tm,), jnp.int32).at[pos.reshape(-1)].set(tok.reshape(-1))

    y = _experts(h2, tile_expert, src, n_used, w_gate, w_up, w_down, layer)
    return _combine_ple(x1, info, p, y, pos[:, 0], pos[:, 1], g_ple, ple_up, ple_gate, g_next, hn_dtype)


def kernel(x_prompt, x_sample, state_ret, cache_swa_k, cache_swa_v, state_pool, p_prompt, p_sample, norm_mix, norm_ffn, norm_ple, norm_final, ret_w_in, ret_gn, ret_w_out, swa_w_in, swa_sink, swa_w_out, pool_w, pool_scale, moe_w_group, moe_b_group, moe_w_router, moe_b_router, moe_w_gate, moe_w_up, moe_w_down, ple_w_up, ple_w_gate):
    bp, lp, d = x_prompt.shape
    bs, ls, _ = x_sample.shape
    tp, ts = bp * lp, bs * ls
    depth = norm_mix.shape[0]
    pd = p_prompt.shape[-1]
    n_ret, _, nh, dk, dv = state_ret.shape
    _, _, win, hk, hd = cache_swa_k.shape
    hq = swa_sink.shape[1]
    assert ls <= CHUNK and ls <= win and lp % CHUNK == 0 and lp >= win and win % CHUNK == 0

    x = jnp.concatenate([x_prompt.reshape(tp, d), x_sample.reshape(ts, d)], axis=0)
    pos = jnp.concatenate([jnp.tile(jnp.arange(lp, dtype=jnp.int32), bp),
                           jnp.tile(PAST_LEN + jnp.arange(ls, dtype=jnp.int32), bs)])
    lg = jnp.log1p(-jnp.exp2(-5.0 - jnp.arange(nh, dtype=F32)))
    ret_cl = _pick(lp, (256, 128, 64))
    tn_d = _pick(d, (512, 256, 128))
    ple_up, ple_gate = ple_w_up.astype(BF16), ple_w_gate.astype(BF16)
    pool_wb = pool_w.astype(BF16)

    hn = _norm(x, norm_mix[0], BF16)
    ret_p = ret_s = None
    k_p, k_s, v_p, v_s, pool_p, pool_s = [], [], [], [], [], []
    for i in range(depth):
        kind, j = i % 3, i // 3
        if kind == 0:
            qkw = 2 * nh * dk + nh * dv
            tn_in = _pick(nh * dk, (1024, 512, 256))
            qkv = _matmul(hn, ret_w_in, j, tn=tn_in, ncols=qkw, rope=(dk, 2 * nh * dk, nh * dk, dk ** -0.5),
                          tables=_rope_tables(pos, dk), out_dtype=BF16, name="ret_in_proj")
            gate = _matmul(hn, ret_w_in, j, tn=tn_in, col0=qkw, name="ret_gate_proj")
            o, ret_p = _retention(qkv, gate, None, ret_gn[j], lg, layer=j, n_layers=n_ret, row0=0, nb=bp, seq=lp,
                                  cl=ret_cl, hb=2, st_buf=ret_p)
            o, ret_s = _retention(qkv, gate, state_ret, ret_gn[j], lg, layer=j, n_layers=n_ret, row0=tp, nb=bs, seq=ls,
                                  cl=ls, hb=nh, o_buf=o, st_buf=ret_s)
            x1 = _matmul(o, ret_w_out, j, tn=tn_d, res=x, name="ret_out_proj")
        elif kind == 1:
            qd, kd = hq * hd, hk * hd
            tn = _pick(kd, (512, 256, 128))
            qkv = _matmul(hn, swa_w_in, j, tn=tn, rope=(hd, qd + kd, 0, 1.0),
                          tables=_rope_tables(pos, hd), name="swa_in_proj")
            o = _swa_prompt(qkv, swa_sink[j], nb=bp, seq=lp, win=win, hq=hq, hk=hk, hd=hd)
            o = _swa_sample(qkv, swa_sink[j], cache_swa_k[j], cache_swa_v[j], o, row0=tp, nb=bs, seq=ls,
                            hq=hq, hk=hk, hd=hd)
            kv_p = jnp.stack([lax.slice(qkv, (b * lp + lp - win, qd), ((b + 1) * lp, qd + 2 * kd)) for b in range(bp)])
            kv_s = lax.slice(qkv, (tp, qd), (tp + ts, qd + 2 * kd)).reshape(bs, ls, 2 * kd)
            k_p.append(kv_p[..., :kd].reshape(bp, win, hk, hd))
            v_p.append(kv_p[..., kd:].reshape(bp, win, hk, hd))
            k_s.append(jnp.concatenate([cache_swa_k[j][:, ls:], kv_s[..., :kd].reshape(bs, ls, hk, hd)], axis=1))
            v_s.append(jnp.concatenate([cache_swa_v[j][:, ls:], kv_s[..., kd:].reshape(bs, ls, hk, hd)], axis=1))
            x1 = _matmul(o, swa_w_out, j, tn=tn_d, res=x, name="swa_out_proj")
        else:
            x1, st_p = _pool(x, norm_mix[i], None, pool_wb[j], pool_scale[j], row0=0, nb=bp, seq=lp, pos0=0)
            x1, st_s = _pool(x, norm_mix[i], state_pool[j], pool_wb[j], pool_scale[j], row0=tp, nb=bs, seq=ls,
                             pos0=PAST_LEN, o_buf=x1)
            pool_p.append(st_p)
            pool_s.append(st_s)
        p = jnp.concatenate([p_prompt[i].reshape(tp, pd), p_sample[i].reshape(ts, pd)], axis=0)
        last = i == depth - 1
        g_next = norm_final if last else norm_mix[i + 1]
        x, hn = _moe_ple(x1, p, i, norm_ffn[i], moe_w_group[i], moe_b_group[i], moe_w_router[i], moe_b_router[i],
                         moe_w_gate, moe_w_up, moe_w_down, norm_ple[i], ple_up[i], ple_gate[i],
                         g_next, F32 if last else BF16)
    y = hn
    return (y[:tp].reshape(bp, lp, d), y[tp:].reshape(bs, ls, d), ret_p, ret_s,
            jnp.stack(k_p), jnp.stack(k_s), jnp.stack(v_p), jnp.stack(v_s), jnp.stack(pool_p), jnp.stack(pool_s))
```
